```python
import jax, jax.numpy as jnp
from jax import lax
import numpy as np

D_MODEL = 2048
BATCH = 8
SEQ = 2048
DEPTH = 1

SSM_D_INNER = D_MODEL
SSM_HEAD_DIM = 64
SSM_HEADS = SSM_D_INNER // SSM_HEAD_DIM
SSM_GROUPS = 8
SSM_HEADS_PER_GROUP = SSM_HEADS // SSM_GROUPS
SSM_STATE = 128
SSM_CONV = 4
SSM_CHUNK = 256
XBC_DIM = SSM_D_INNER + 2 * SSM_GROUPS * SSM_STATE
CFM_D = D_MODEL
CFM_WIDTH = 31
D_FF = 5632
FFN_CONV = 3
N_BRANCHES = 2
IN_DIM = SSM_D_INNER + XBC_DIM + SSM_HEADS + 2 * CFM_D + N_BRANCHES * D_MODEL
IN_SPLITS = (SSM_D_INNER,
             SSM_D_INNER + XBC_DIM,
             SSM_D_INNER + XBC_DIM + SSM_HEADS,
             SSM_D_INNER + XBC_DIM + SSM_HEADS + 2 * CFM_D)
RMS_EPS = 1e-6
LN_EPS = 1e-5
N_MOD = 6

kernel_name = "hybrid_ssd_conformer_convffn_block"


def rms_norm(x, g):
    xf = x.astype(jnp.float32)
    y = xf * lax.rsqrt(jnp.mean(xf * xf, axis=-1, keepdims=True) + RMS_EPS)
    return (y * g.astype(jnp.float32)).astype(x.dtype)


def layer_norm(x, g, b):
    xf = x.astype(jnp.float32)
    mu = jnp.mean(xf, axis=-1, keepdims=True)
    var = jnp.mean(jnp.square(xf - mu), axis=-1, keepdims=True)
    y = (xf - mu) * lax.rsqrt(var + LN_EPS)
    return (y * g.astype(jnp.float32) + b.astype(jnp.float32)).astype(x.dtype)


def gated_group_rms_norm(y, z, g):
    b, l, d = y.shape
    yg = (y * jax.nn.silu(z)).astype(jnp.float32).reshape(b, l, SSM_GROUPS, d // SSM_GROUPS)
    yg = yg * lax.rsqrt(jnp.mean(yg * yg, axis=-1, keepdims=True) + RMS_EPS)
    return (yg.reshape(b, l, d) * g.astype(jnp.float32)).astype(y.dtype)


def modulate(h, shift, scale):
    return h * (1.0 + scale[:, None, :]) + shift[:, None, :]


def causal_dwconv(x, w, bias):
    k, ch = w.shape
    y = lax.conv_general_dilated(
        x, w[:, None, :].astype(x.dtype), window_strides=(1,), padding=[(k - 1, 0)],
        dimension_numbers=("NWC", "WIO", "NWC"), feature_group_count=ch)
    return y + bias


def ssd_chunked(x, dt, a, bm, cm, d_skip):
    b, l = x.shape[:2]
    pad = (-l) % SSM_CHUNK
    if pad:
        padw = lambda t: jnp.pad(t, [(0, 0), (0, pad)] + [(0, 0)] * (t.ndim - 2))
        x, dt, bm, cm = padw(x), padw(dt), padw(bm), padw(cm)
    nc = (l + pad) // SSM_CHUNK
    G, R, P, N, Q = SSM_GROUPS, SSM_HEADS_PER_GROUP, SSM_HEAD_DIM, SSM_STATE, SSM_CHUNK
    xc = x.reshape(b, nc, Q, G, R, P)
    dtc = dt.reshape(b, nc, Q, G, R)
    bc = bm.reshape(b, nc, Q, G, N)
    cc = cm.reshape(b, nc, Q, G, N)
    a_cum = jnp.cumsum(dtc * a.reshape(G, R), axis=2)
    xdt = xc * dtc[..., None]
    causal = jnp.tril(jnp.ones((Q, Q), dtype=bool))[None, None, :, :, None, None]
    seg = a_cum[:, :, :, None] - a_cum[:, :, None, :]
    decay = jnp.exp(jnp.where(causal, seg, -jnp.inf))
    cb = jnp.einsum("bclgn,bcsgn->bclsg", cc, bc)
    y_diag = jnp.einsum("bclsgr,bcsgrp->bclgrp", decay * cb[..., None], xdt)
    decay_states = jnp.exp(a_cum[:, :, -1:] - a_cum)
    states = jnp.einsum("bclgn,bclgr,bclgrp->bcgrpn", bc, decay_states, xdt)
    chunk_decay = jnp.exp(a_cum[:, :, -1])

    def step(h, inp):
        s, dcy = inp
        return h * dcy[..., None, None] + s, h

    h0 = jnp.zeros((b, G, R, P, N), dtype=x.dtype)
    _, prev = lax.scan(step, h0, (jnp.moveaxis(states, 1, 0), jnp.moveaxis(chunk_decay, 1, 0)))
    prev = jnp.moveaxis(prev, 0, 1)
    y_off = jnp.einsum("bclgn,bcgrpn,bclgr->bclgrp", cc, prev, jnp.exp(a_cum))
    y = y_diag + y_off + xc * d_skip.reshape(G, R)[..., None]
    return y.reshape(b, nc * Q, G * R * P)[:, :l]


def hybrid_layer(x, c, w_ada, b_ada, g_norm1, w_in, w_ssm_conv, b_ssm_conv, dt_bias,
                 a_log, d_skip, g_ssm_norm, w_ssm_out, w_cfm_dw, b_cfm_dw, g_cfm_ln,
                 b_cfm_ln, w_cfm_out, b_cfm_out, w_o, g_norm2, w_up, w_ff_conv,
                 b_ff_conv, w_down):
    b, l, _ = x.shape
    mod = jax.nn.silu(c) @ w_ada + b_ada
    sh1, sc1, gt1, sh2, sc2, gt2 = jnp.split(mod, N_MOD, axis=-1)

    h = modulate(rms_norm(x, g_norm1), sh1, sc1)
    proj = h @ w_in
    z, xbc, dt_raw, cfm_in, gate_logits = jnp.split(proj, IN_SPLITS, axis=-1)

    xbc = jax.nn.silu(causal_dwconv(xbc, w_ssm_conv, b_ssm_conv))
    xs, bm, cm = jnp.split(xbc, (SSM_D_INNER, SSM_D_INNER + SSM_GROUPS * SSM_STATE), axis=-1)
    f32 = jnp.float32
    dt = jax.nn.softplus(dt_raw.astype(f32) + dt_bias.astype(f32))
    a = -jnp.exp(a_log.astype(f32))
    y = ssd_chunked(xs.reshape(b, l, SSM_HEADS, SSM_HEAD_DIM).astype(f32), dt, a,
                    bm.reshape(b, l, SSM_GROUPS, SSM_STATE).astype(f32),
                    cm.reshape(b, l, SSM_GROUPS, SSM_STATE).astype(f32),
                    d_skip.astype(f32)).astype(x.dtype)
    y_a = gated_group_rms_norm(y, z, g_ssm_norm) @ w_ssm_out

    u_lin, u_gate = jnp.split(cfm_in, 2, axis=-1)
    u = u_lin * jax.nn.sigmoid(u_gate)
    u = causal_dwconv(u, w_cfm_dw, b_cfm_dw)
    u = jax.nn.silu(layer_norm(u, g_cfm_ln, b_cfm_ln))
    y_b = u @ w_cfm_out + b_cfm_out

    g_a, g_b = jnp.split(jax.nn.sigmoid(gate_logits), N_BRANCHES, axis=-1)
    merged = g_a * y_a + g_b * y_b
    x = x + gt1[:, None, :] * (merged @ w_o)

    h2 = modulate(rms_norm(x, g_norm2), sh2, sc2)
    u_ff, v_ff = jnp.split(h2 @ w_up, 2, axis=-1)
    u_ff = causal_dwconv(u_ff, w_ff_conv, b_ff_conv)
    x = x + gt2[:, None, :] * ((jax.nn.silu(u_ff) * v_ff) @ w_down)
    return x


def setup_inputs(seed: int = 0) -> dict:
    key = jax.random.key(seed)
    ks = iter(jax.random.split(key, 40))
    nrm = lambda shape, scale: jax.random.normal(next(ks), shape, jnp.float32) * scale
    L = DEPTH
    dt0 = jnp.exp(jax.random.uniform(next(ks), (L, SSM_HEADS), jnp.float32,
                                     np.log(1e-3).astype(np.float32), np.log(1e-1).astype(np.float32)))
    dt_bias = dt0 + jnp.log(-jnp.expm1(-dt0))
    a_log = jnp.log(jax.random.uniform(next(ks), (L, SSM_HEADS), jnp.float32, 1.0, 16.0))
    return {
        "x": nrm((BATCH, SEQ, D_MODEL), 1.0),
        "c": nrm((BATCH, D_MODEL), 1.0),
        "w_ada": nrm((L, D_MODEL, N_MOD * D_MODEL), D_MODEL ** -0.5),
        "b_ada": nrm((L, N_MOD * D_MODEL), 0.02),
        "g_norm1": 1.0 + nrm((L, D_MODEL), 0.02),
        "w_in": nrm((L, D_MODEL, IN_DIM), D_MODEL ** -0.5),
        "w_ssm_conv": nrm((L, SSM_CONV, XBC_DIM), SSM_CONV ** -0.5),
        "b_ssm_conv": nrm((L, XBC_DIM), 0.02),
        "dt_bias": dt_bias,
        "a_log": a_log,
        "d_skip": 1.0 + nrm((L, SSM_HEADS), 0.02),
        "g_ssm_norm": 1.0 + nrm((L, SSM_D_INNER), 0.02),
        "w_ssm_out": nrm((L, SSM_D_INNER, D_MODEL), SSM_D_INNER ** -0.5),
        "w_cfm_dw": nrm((L, CFM_WIDTH, CFM_D), CFM_WIDTH ** -0.5),
        "b_cfm_dw": nrm((L, CFM_D), 0.02),
        "g_cfm_ln": 1.0 + nrm((L, CFM_D), 0.02),
        "b_cfm_ln": nrm((L, CFM_D), 0.02),
        "w_cfm_out": nrm((L, CFM_D, D_MODEL), CFM_D ** -0.5),
        "b_cfm_out": nrm((L, D_MODEL), 0.02),
        "w_o": nrm((L, D_MODEL, D_MODEL), D_MODEL ** -0.5),
        "g_norm2": 1.0 + nrm((L, D_MODEL), 0.02),
        "w_up": nrm((L, D_MODEL, 2 * D_FF), D_MODEL ** -0.5),
        "w_ff_conv": nrm((L, FFN_CONV, D_FF), FFN_CONV ** -0.5),
        "b_ff_conv": nrm((L, D_FF), 0.02),
        "w_down": nrm((L, D_FF, D_MODEL), D_FF ** -0.5),
        "g_final": 1.0 + nrm((D_MODEL,), 0.02),
    }


def reference(x, c, w_ada, b_ada, g_norm1, w_in, w_ssm_conv, b_ssm_conv, dt_bias, a_log,
              d_skip, g_ssm_norm, w_ssm_out, w_cfm_dw, b_cfm_dw, g_cfm_ln, b_cfm_ln,
              w_cfm_out, b_cfm_out, w_o, g_norm2, w_up, w_ff_conv, b_ff_conv, w_down,
              g_final):
    for i in range(DEPTH):
        x = hybrid_layer(x, c, w_ada[i], b_ada[i], g_norm1[i], w_in[i], w_ssm_conv[i],
                         b_ssm_conv[i], dt_bias[i], a_log[i], d_skip[i], g_ssm_norm[i],
                         w_ssm_out[i], w_cfm_dw[i], b_cfm_dw[i], g_cfm_ln[i], b_cfm_ln[i],
                         w_cfm_out[i], b_cfm_out[i], w_o[i], g_norm2[i], w_up[i],
                         w_ff_conv[i], b_ff_conv[i], w_down[i])
    return rms_norm(x, g_final)
```

```python
import functools

import jax
import jax.numpy as jnp
from jax import lax
from jax.experimental import pallas as pl
from jax.experimental.pallas import tpu as pltpu

F32 = jnp.float32
BF16 = jnp.bfloat16

SSM_HEAD_DIM = 64
SSM_GROUPS = 8
SSM_STATE = 128
SSM_CHUNK = 256
N_MOD = 6
RMS_EPS = 1e-6
LN_EPS = 1e-5

LANES = 128
SUBLANES = 8
CONV_PAD = 32
CONV_ROWS = 32
VMEM_LIMIT = 56 * 1024 * 1024


def _params(*sem):
    return pltpu.CompilerParams(dimension_semantics=sem, vmem_limit_bytes=VMEM_LIMIT)


def _sigmoid(v):
    return 1.0 / (1.0 + jnp.exp(-v))


def _silu(v):
    return v * _sigmoid(v)


def _softplus(v):
    return jnp.maximum(v, 0.0) + jnp.log1p(jnp.exp(-jnp.abs(v)))


def _rms_rows(v):
    return v * lax.rsqrt(jnp.mean(v * v, axis=-1, keepdims=True) + RMS_EPS)


def _ada_body(c_ref, w_ref, b_ref, o_ref):
    s = _silu(c_ref[...]).astype(BF16)
    o_ref[...] = jnp.dot(s, w_ref[...].astype(BF16), preferred_element_type=F32) + b_ref[...]


def _ada(c, w, b, bn=1024):
    bsz, d = c.shape
    n = w.shape[1]
    return pl.pallas_call(
        _ada_body,
        grid=(n // bn,),
        in_specs=[pl.BlockSpec((bsz, d), lambda j: (0, 0)),
                  pl.BlockSpec((d, bn), lambda j: (0, j)),
                  pl.BlockSpec((1, bn), lambda j: (0, j))],
        out_specs=pl.BlockSpec((bsz, bn), lambda j: (0, j)),
        out_shape=jax.ShapeDtypeStruct((bsz, n), F32),
        compiler_params=_params("arbitrary"),
        name="ada",
    )(c, w, b)


def _norm1_body(x_ref, g_ref, mod_ref, o_ref):
    shift = mod_ref[0, 0:1, :]
    scale = mod_ref[0, 1:2, :]
    y = _rms_rows(x_ref[...]) * g_ref[...]
    o_ref[...] = (y * (1.0 + scale) + shift).astype(BF16)


def _norm1(x2, g, mod3, seq, bm=512):
    m, d = x2.shape
    per = seq // bm
    return pl.pallas_call(
        _norm1_body,
        grid=(m // bm,),
        in_specs=[pl.BlockSpec((bm, d), lambda i: (i, 0)),
                  pl.BlockSpec((1, d), lambda i: (0, 0)),
                  pl.BlockSpec((1, N_MOD, d), lambda i: (i // per, 0, 0))],
        out_specs=pl.BlockSpec((bm, d), lambda i: (i, 0)),
        out_shape=jax.ShapeDtypeStruct((m, d), BF16),
        compiler_params=_params("parallel"),
        name="norm1",
    )(x2, g, mod3)


def _row_chunks(seq, rows, fn):
    def step(i, carry):
        fn(pl.multiple_of(i * rows, rows))
        return carry
    lax.fori_loop(0, seq // rows, step, 0)


def _causal_conv(acc_ref, w_ref, b_ref, width, t0, rows):
    acc = None
    win = acc_ref[pl.ds(t0, CONV_PAD + rows), :]
    for k in range(width):
        lo = CONV_PAD - (width - 1) + k
        xs = win[lo:lo + rows, :]
        term = w_ref[k:k + 1, :] * xs
        acc = term if acc is None else acc + term
    return acc + b_ref[...]


def _seqmm_body(nw, nc, seq, epilogue, *refs):
    h_ref = refs[0]
    w_refs = refs[1:1 + nw]
    c_refs = refs[1 + nw:1 + nw + nc]
    o_ref = refs[1 + nw + nc]
    acc_refs = refs[2 + nw + nc:]
    for w_ref, acc_ref in zip(w_refs, acc_refs):
        acc_ref[0:CONV_PAD, :] = jnp.zeros((CONV_PAD, acc_ref.shape[1]), F32)
        acc_ref[CONV_PAD:CONV_PAD + seq, :] = jnp.dot(
            h_ref[...], w_ref[...], preferred_element_type=F32)
    epilogue(acc_refs, c_refs, o_ref)


def _seqmm(h, ws, consts, epilogue, out_dtype, seq, bn, name):
    m, k = h.shape
    n = ws[0].shape[1]
    nw, nc = len(ws), len(consts)
    in_specs = [pl.BlockSpec((seq, k), lambda b, j: (b, 0))]
    in_specs += [pl.BlockSpec((k, bn), lambda b, j: (0, j)) for _ in ws]
    in_specs += [pl.BlockSpec((cst.shape[0], bn), lambda b, j: (0, j)) for cst in consts]
    return pl.pallas_call(
        functools.partial(_seqmm_body, nw, nc, seq, epilogue),
        grid=(m // seq, n // bn),
        in_specs=in_specs,
        out_specs=pl.BlockSpec((seq, bn), lambda b, j: (b, j)),
        out_shape=jax.ShapeDtypeStruct((m, n), out_dtype),
        scratch_shapes=[pltpu.VMEM((CONV_PAD + seq, bn), F32) for _ in ws],
        compiler_params=_params("parallel", "arbitrary"),
        name=name,
    )(h, *ws, *consts)


def _ep_silu(seq, acc_refs, c_refs, o_ref):
    (acc,) = acc_refs

    def chunk(t0):
        v = acc[pl.ds(t0 + CONV_PAD, CONV_ROWS), :]
        o_ref[pl.ds(t0, CONV_ROWS), :] = _silu(v).astype(o_ref.dtype)
    _row_chunks(seq, CONV_ROWS, chunk)


def _ep_sigmoid(seq, acc_refs, c_refs, o_ref):
    (acc,) = acc_refs

    def chunk(t0):
        v = acc[pl.ds(t0 + CONV_PAD, CONV_ROWS), :]
        o_ref[pl.ds(t0, CONV_ROWS), :] = _sigmoid(v).astype(o_ref.dtype)
    _row_chunks(seq, CONV_ROWS, chunk)


def _ep_softplus_bias(seq, acc_refs, c_refs, o_ref):
    (acc,) = acc_refs
    (bias,) = c_refs

    def chunk(t0):
        v = acc[pl.ds(t0 + CONV_PAD, CONV_ROWS), :] + bias[...]
        o_ref[pl.ds(t0, CONV_ROWS), :] = _softplus(v).astype(o_ref.dtype)
    _row_chunks(seq, CONV_ROWS, chunk)


def _ep_conv_silu(seq, width, acc_refs, c_refs, o_ref):
    (acc,) = acc_refs
    w_ref, b_ref = c_refs

    def chunk(t0):
        y = _causal_conv(acc, w_ref, b_ref, width, t0, CONV_ROWS)
        o_ref[pl.ds(t0, CONV_ROWS), :] = _silu(y).astype(o_ref.dtype)
    _row_chunks(seq, CONV_ROWS, chunk)


def _ep_glu_conv(seq, width, acc_refs, c_refs, o_ref):
    lin, gate = acc_refs
    w_ref, b_ref = c_refs

    def glu(t0):
        rows = pl.ds(t0 + CONV_PAD, CONV_ROWS)
        lin[rows, :] = lin[rows, :] * _sigmoid(gate[rows, :])
    _row_chunks(seq, CONV_ROWS, glu)

    def chunk(t0):
        y = _causal_conv(lin, w_ref, b_ref, width, t0, CONV_ROWS)
        o_ref[pl.ds(t0, CONV_ROWS), :] = y.astype(o_ref.dtype)
    _row_chunks(seq, CONV_ROWS, chunk)


def _ep_conv_silu_gate(seq, width, acc_refs, c_refs, o_ref):
    u, v = acc_refs
    w_ref, b_ref = c_refs

    def chunk(t0):
        y = _causal_conv(u, w_ref, b_ref, width, t0, CONV_ROWS)
        o_ref[pl.ds(t0, CONV_ROWS), :] = (
            _silu(y) * v[pl.ds(t0 + CONV_PAD, CONV_ROWS), :]).astype(o_ref.dtype)
    _row_chunks(seq, CONV_ROWS, chunk)


def _ssd_body(x_ref, b_ref, c_ref, dt_ref, z_ref, alog_ref, dskip_ref, gn_ref, o_ref, state_ref):
    q = SSM_CHUNK
    p = SSM_HEAD_DIM
    n = SSM_STATE
    heads_per_group = x_ref.shape[1] // (SSM_GROUPS * p)
    gw = heads_per_group * p

    @pl.when(pl.program_id(1) == 0)
    def _():
        state_ref[...] = jnp.zeros_like(state_ref)

    dt = dt_ref[...]
    dta = dt * (-jnp.exp(alog_ref[...]))
    row = lax.broadcasted_iota(jnp.int32, (q, q), 0)
    col = lax.broadcasted_iota(jnp.int32, (q, q), 1)
    causal = row >= col
    a_cum = jnp.dot(causal.astype(F32), dta, precision=lax.Precision.HIGHEST,
                    preferred_element_type=F32)
    a_cum_t = a_cum.T
    a_last = a_cum[q - 1:q, :]
    e_cum = jnp.exp(a_cum)
    dt_end = dt * jnp.exp(a_last - a_cum)
    chunk_decay = jnp.exp(a_last)

    for g in range(SSM_GROUPS):
        bg = b_ref[:, g * n:(g + 1) * n]
        cg = c_ref[:, g * n:(g + 1) * n]
        cb = lax.dot_general(cg, bg, (((1,), (1,)), ((), ())), preferred_element_type=F32)
        prev = state_ref[g]
        c_prev = jnp.dot(cg, prev.astype(BF16), preferred_element_type=F32)
        ys, xends, decays = [], [], []
        for r in range(heads_per_group):
            hd = g * heads_per_group + r
            xh = x_ref[:, hd * p:(hd + 1) * p].astype(F32)
            seg = a_cum[:, hd:hd + 1] - a_cum_t[hd:hd + 1, :]
            decay = jnp.exp(jnp.where(causal, seg, -jnp.inf))
            y_diag = jnp.dot((decay * cb).astype(BF16), (xh * dt[:, hd:hd + 1]).astype(BF16),
                             preferred_element_type=F32)
            y_off = c_prev[:, r * p:(r + 1) * p] * e_cum[:, hd:hd + 1]
            ys.append(y_diag + y_off + xh * dskip_ref[:, hd * p:(hd + 1) * p])
            xends.append((xh * dt_end[:, hd:hd + 1]).astype(BF16))
            decays.append(jnp.broadcast_to(chunk_decay[:, hd:hd + 1], (1, p)))
        new = lax.dot_general(bg, jnp.concatenate(xends, axis=1), (((0,), (0,)), ((), ())),
                              preferred_element_type=F32)
        state_ref[g] = prev * jnp.concatenate(decays, axis=1) + new
        cols = slice(g * gw, (g + 1) * gw)
        yz = jnp.concatenate(ys, axis=1) * z_ref[:, cols].astype(F32)
        o_ref[:, cols] = (_rms_rows(yz) * gn_ref[:, cols]).astype(o_ref.dtype)


def _ssd(xbc, dt, zs, a_log_pad, dskip_cols, g_norm, seq):
    m = xbc.shape[0]
    d_inner = zs.shape[1]
    bc_w = SSM_GROUPS * SSM_STATE
    nchunk = seq // SSM_CHUNK
    q = SSM_CHUNK
    rows = lambda b, c: b * nchunk + c
    return pl.pallas_call(
        _ssd_body,
        grid=(m // seq, nchunk),
        in_specs=[pl.BlockSpec((q, d_inner), lambda b, c: (rows(b, c), 0)),
                  pl.BlockSpec((q, bc_w), lambda b, c: (rows(b, c), d_inner // bc_w)),
                  pl.BlockSpec((q, bc_w), lambda b, c: (rows(b, c), d_inner // bc_w + 1)),
                  pl.BlockSpec((q, LANES), lambda b, c: (rows(b, c), 0)),
                  pl.BlockSpec((q, d_inner), lambda b, c: (rows(b, c), 0)),
                  pl.BlockSpec((1, LANES), lambda b, c: (0, 0)),
                  pl.BlockSpec((1, d_inner), lambda b, c: (0, 0)),
                  pl.BlockSpec((1, d_inner), lambda b, c: (0, 0))],
        out_specs=pl.BlockSpec((q, d_inner), lambda b, c: (rows(b, c), 0)),
        out_shape=jax.ShapeDtypeStruct((m, d_inner), BF16),
        scratch_shapes=[pltpu.VMEM((SSM_GROUPS, SSM_STATE, d_inner // SSM_GROUPS), F32)],
        compiler_params=_params("parallel", "arbitrary"),
        name="ssd",
    )(xbc, xbc, xbc, dt, zs, a_log_pad, dskip_cols, g_norm)


def _merge_body(yn_ref, uc_ref, ga_ref, gb_ref, wa_ref, wb_ref, lng_ref, lnb_ref, bo_ref, o_ref):
    u = uc_ref[...].astype(F32)
    mu = jnp.mean(u, axis=-1, keepdims=True)
    ctr = u - mu
    var = jnp.mean(ctr * ctr, axis=-1, keepdims=True)
    un = _silu(ctr * lax.rsqrt(var + LN_EPS) * lng_ref[...] + lnb_ref[...]).astype(BF16)
    ya = jnp.dot(yn_ref[...], wa_ref[...], preferred_element_type=F32)
    yb = jnp.dot(un, wb_ref[...], preferred_element_type=F32) + bo_ref[...]
    o_ref[...] = (ga_ref[...].astype(F32) * ya + gb_ref[...].astype(F32) * yb).astype(BF16)


def _merge(yn, uc, gates, wa, wb, ln_g, ln_b, b_out, bm=512):
    m, d = yn.shape
    row = pl.BlockSpec((bm, d), lambda i: (i, 0))
    vec = pl.BlockSpec((1, d), lambda i: (0, 0))
    wspec = pl.BlockSpec((d, d), lambda i: (0, 0), pipeline_mode=pl.Buffered(1))
    return pl.pallas_call(
        _merge_body,
        grid=(m // bm,),
        in_specs=[row, row, row, pl.BlockSpec((bm, d), lambda i: (i, 1)), wspec, wspec,
                  vec, vec, vec],
        out_specs=row,
        out_shape=jax.ShapeDtypeStruct((m, d), BF16),
        compiler_params=_params("parallel"),
        name="merge",
    )(yn, uc, gates, gates, wa, wb, ln_g, ln_b, b_out)


def _oproj_body(mg_ref, x_ref, w_ref, g2_ref, mod_ref, x1_ref, h2_ref):
    gate = mod_ref[0, 2:3, :]
    shift = mod_ref[0, 3:4, :]
    scale = mod_ref[0, 4:5, :]
    x1 = x_ref[...] + gate * jnp.dot(mg_ref[...], w_ref[...], preferred_element_type=F32)
    x1_ref[...] = x1
    h2_ref[...] = (_rms_rows(x1) * g2_ref[...] * (1.0 + scale) + shift).astype(BF16)


def _oproj(merged, x2, w_o, g2, mod3, seq, bm=512):
    m, d = x2.shape
    per = seq // bm
    row = pl.BlockSpec((bm, d), lambda i: (i, 0))
    return pl.pallas_call(
        _oproj_body,
        grid=(m // bm,),
        in_specs=[row, row,
                  pl.BlockSpec((d, d), lambda i: (0, 0), pipeline_mode=pl.Buffered(1)),
                  pl.BlockSpec((1, d), lambda i: (0, 0)),
                  pl.BlockSpec((1, N_MOD, d), lambda i: (i // per, 0, 0))],
        out_specs=[row, row],
        out_shape=[jax.ShapeDtypeStruct((m, d), F32), jax.ShapeDtypeStruct((m, d), BF16)],
        compiler_params=_params("parallel"),
        name="oproj",
    )(merged, x2, w_o, g2, mod3)


def _down_body(g_ref, x1_ref, w_ref, gf_ref, mod_ref, o_ref):
    gate = mod_ref[0, 5:6, :]
    x2 = x1_ref[...] + gate * jnp.dot(g_ref[...], w_ref[...], preferred_element_type=F32)
    o_ref[...] = _rms_rows(x2) * gf_ref[...]


def _down(gff, x1, w_down, g_final, mod3, seq, bm=256):
    m, d = x1.shape
    dff = gff.shape[1]
    per = seq // bm
    row = pl.BlockSpec((bm, d), lambda i: (i, 0))
    return pl.pallas_call(
        _down_body,
        grid=(m // bm,),
        in_specs=[pl.BlockSpec((bm, dff), lambda i: (i, 0)), row,
                  pl.BlockSpec((dff, d), lambda i: (0, 0), pipeline_mode=pl.Buffered(1)),
                  pl.BlockSpec((1, d), lambda i: (0, 0)),
                  pl.BlockSpec((1, N_MOD, d), lambda i: (i // per, 0, 0))],
        out_specs=row,
        out_shape=jax.ShapeDtypeStruct((m, d), F32),
        compiler_params=_params("parallel"),
        name="down",
    )(gff, x1, w_down, g_final, mod3)


def _layer(x, c, w_ada, b_ada, g_norm1, w_in, w_ssm_conv, b_ssm_conv, dt_bias, a_log, d_skip,
           g_ssm_norm, w_ssm_out, w_cfm_dw, b_cfm_dw, g_cfm_ln, b_cfm_ln, w_cfm_out, b_cfm_out,
           w_o, g_norm2, w_up, w_ff_conv, b_ff_conv, w_down, g_final):
    bsz, seq, d = x.shape
    d_inner = w_ssm_out.shape[0]
    heads = dt_bias.shape[0]
    xbc_dim = w_ssm_conv.shape[1]
    cfm_d = w_cfm_out.shape[0]
    d_ff = w_down.shape[0]
    row = lambda v: v.reshape(1, -1).astype(F32)

    o_xbc = d_inner
    o_dt = o_xbc + xbc_dim
    o_lin = o_dt + heads
    o_gate = o_lin + cfm_d
    o_g = o_gate + cfm_d
    wb = lambda lo, hi: w_in[:, lo:hi].astype(BF16)
    w_dt = jnp.pad(w_in[:, o_dt:o_lin], ((0, 0), (0, LANES - heads))).astype(BF16)
    dt_bias_pad = jnp.pad(row(dt_bias), ((0, 0), (0, LANES - heads)))
    a_log_pad = jnp.pad(row(a_log), ((0, 0), (0, LANES - heads)))
    dskip_cols = jnp.repeat(d_skip.astype(F32), SSM_HEAD_DIM).reshape(1, d_inner)

    x2 = x.reshape(bsz * seq, d)
    mod3 = _ada(c, w_ada, row(b_ada)).reshape(bsz, N_MOD, d)
    h = _norm1(x2, row(g_norm1), mod3, seq)

    mm = functools.partial(_seqmm, h, seq=seq)
    zs = mm([wb(0, o_xbc)], [], functools.partial(_ep_silu, seq), BF16, bn=512, name="in_z")
    xbc = mm([wb(o_xbc, o_dt)], [w_ssm_conv.astype(F32), row(b_ssm_conv)],
             functools.partial(_ep_conv_silu, seq, w_ssm_conv.shape[0]), BF16, bn=512,
             name="in_xbc")
    dt = mm([w_dt], [dt_bias_pad], functools.partial(_ep_softplus_bias, seq), F32, bn=LANES,
            name="in_dt")
    uc = mm([wb(o_lin, o_gate), wb(o_gate, o_g)], [w_cfm_dw.astype(F32), row(b_cfm_dw)],
            functools.partial(_ep_glu_conv, seq, w_cfm_dw.shape[0]), BF16, bn=512, name="in_cfm")
    gates = mm([wb(o_g, w_in.shape[1])], [], functools.partial(_ep_sigmoid, seq), BF16, bn=512,
               name="in_gates")

    yn = _ssd(xbc, dt, zs, a_log_pad, dskip_cols, row(g_ssm_norm), seq)
    merged = _merge(yn, uc, gates, w_ssm_out.astype(BF16), w_cfm_out.astype(BF16),
                    row(g_cfm_ln), row(b_cfm_ln), row(b_cfm_out))
    x1, h2 = _oproj(merged, x2, w_o.astype(BF16), row(g_norm2), mod3, seq)

    gff = _seqmm(h2, [w_up[:, :d_ff].astype(BF16), w_up[:, d_ff:].astype(BF16)],
                 [w_ff_conv.astype(F32), row(b_ff_conv)],
                 functools.partial(_ep_conv_silu_gate, seq, w_ff_conv.shape[0]), BF16,
                 seq=seq, bn=512, name="ffn_up")
    out = _down(gff, x1, w_down.astype(BF16), row(g_final), mod3, seq)
    return out.reshape(bsz, seq, d)


def kernel(x, c, w_ada, b_ada, g_norm1, w_in, w_ssm_conv, b_ssm_conv, dt_bias, a_log, d_skip, g_ssm_norm, w_ssm_out, w_cfm_dw, b_cfm_dw, g_cfm_ln, b_cfm_ln, w_cfm_out, b_cfm_out, w_o, g_norm2, w_up, w_ff_conv, b_ff_conv, w_down, g_final):
    assert w_ada.shape[0] == 1, "single-layer stack"
    return _layer(x, c, w_ada[0], b_ada[0], g_norm1[0], w_in[0], w_ssm_conv[0], b_ssm_conv[0],
                  dt_bias[0], a_log[0], d_skip[0], g_ssm_norm[0], w_ssm_out[0], w_cfm_dw[0],
                  b_cfm_dw[0], g_cfm_ln[0], b_cfm_ln[0], w_cfm_out[0], b_cfm_out[0], w_o[0],
                  g_norm2[0], w_up[0], w_ff_conv[0], b_ff_conv[0], w_down[0], g_final)
```

```python
import functools

import jax
import jax.numpy as jnp
from jax import lax
from jax.experimental import pallas as pl
from jax.experimental.pallas import tpu as pltpu

F32 = jnp.float32
BF16 = jnp.bfloat16

SSM_HEAD_DIM = 64
SSM_GROUPS = 8
SSM_STATE = 128
SSM_CHUNK = 256
N_MOD = 6
RMS_EPS = 1e-6
LN_EPS = 1e-5

LANES = 128
SUBLANES = 8
CHUNK = SSM_CHUNK
HALO = CHUNK
SUB_ROWS = 64
SUB_COLS = 256
VMEM_LIMIT = 56 * 1024 * 1024


def _params(*sem, flags=None):
    return pltpu.CompilerParams(dimension_semantics=sem, vmem_limit_bytes=VMEM_LIMIT, flags=flags)


def _sigmoid(v):
    return 1.0 / (1.0 + jnp.exp(-v))


def _silu(v):
    return v * _sigmoid(v)


def _softplus(v):
    return jnp.maximum(v, 0.0) + jnp.log1p(jnp.exp(-jnp.abs(v)))


def _rms_rows(v):
    return v * lax.rsqrt(jnp.mean(v * v, axis=-1, keepdims=True) + RMS_EPS)


def _ada_body(c_ref, w_ref, b_ref, o_ref):
    s = _silu(c_ref[...]).astype(BF16)
    o_ref[...] = jnp.dot(s, w_ref[...].astype(BF16), preferred_element_type=F32) + b_ref[...]


def _ada(c, w, b, bn=1024):
    bsz, d = c.shape
    n = w.shape[1]
    return pl.pallas_call(
        _ada_body,
        grid=(n // bn,),
        in_specs=[pl.BlockSpec((bsz, d), lambda j: (0, 0)),
                  pl.BlockSpec((d, bn), lambda j: (0, j)),
                  pl.BlockSpec((1, bn), lambda j: (0, j))],
        out_specs=pl.BlockSpec((bsz, bn), lambda j: (0, j)),
        out_shape=jax.ShapeDtypeStruct((bsz, n), F32),
        compiler_params=_params("arbitrary"),
        name="ada",
    )(c, w, b)


def _norm1_body(x_ref, g_ref, mod_ref, o_ref):
    shift = mod_ref[0, 0:1, :]
    scale = mod_ref[0, 1:2, :]
    y = _rms_rows(x_ref[...]) * g_ref[...]
    o_ref[...] = (y * (1.0 + scale) + shift).astype(BF16)


def _norm1(x2, g, mod3, seq):
    m, d = x2.shape
    nch = seq // CHUNK
    out = pl.pallas_call(
        _norm1_body,
        grid=(m // seq, nch),
        in_specs=[pl.BlockSpec((CHUNK, d), lambda b, s: (b * nch + s, 0)),
                  pl.BlockSpec((1, d), lambda b, s: (0, 0)),
                  pl.BlockSpec((1, N_MOD, d), lambda b, s: (b, 0, 0))],
        out_specs=pl.BlockSpec((CHUNK, d), lambda b, s: (b, s)),
        out_shape=jax.ShapeDtypeStruct((m // nch, nch * d), BF16),
        compiler_params=_params("parallel", "parallel"),
        name="norm1",
    )(x2, g, mod3)
    return out.reshape(m, d)


def _seq_specs(h, ws, consts, seq, bn):
    k = h.shape[1]
    in_specs = [pl.BlockSpec((seq, k), lambda b, j: (b, 0))]
    in_specs += [pl.BlockSpec((k, bn), lambda b, j: (0, j)) for _ in ws]
    in_specs += [pl.BlockSpec((cst.shape[0], bn), lambda b, j: (0, j)) for cst in consts]
    return in_specs


def _seqmm_ew_body(seq, fn, has_bias, h_ref, w_ref, *refs):
    o_ref = refs[-1]
    for i in range(seq // CHUNK):
        rows = slice(i * CHUNK, (i + 1) * CHUNK)
        a = jnp.dot(h_ref[rows, :], w_ref[...], preferred_element_type=F32)
        if has_bias:
            a = a + refs[0][...]
        o_ref[rows, :] = fn(a).astype(o_ref.dtype)


def _seqmm_ew(h, w, bias, fn, out_dtype, seq, bn, name):
    m = h.shape[0]
    n = w.shape[1]
    consts = [] if bias is None else [bias]
    return pl.pallas_call(
        functools.partial(_seqmm_ew_body, seq, fn, bias is not None),
        grid=(m // seq, n // bn),
        in_specs=_seq_specs(h, [w], consts, seq, bn),
        out_specs=pl.BlockSpec((seq, bn), lambda b, j: (b, j)),
        out_shape=jax.ShapeDtypeStruct((m, n), out_dtype),
        compiler_params=_params("parallel", "arbitrary"),
        name=name,
    )(h, w, *consts)


def _seqmm_conv_body(seq, width, kind, h_ref, w0_ref, *refs):
    two = kind in ("glu", "gate")
    w1_ref = refs[0] if two else None
    cw_ref, cb_ref, o_ref, u_ref = refs[two:two + 4]
    ustage_ref = refs[two + 4]
    v_ref, vstage_ref = refs[two + 5:two + 7] if kind == "gate" else (None, None)
    nch = seq // CHUNK
    bn = o_ref.shape[1]

    def dot_chunk(r, u_dst, v_dst):
        lhs = h_ref[pl.ds(r, CHUNK), :]
        a = jnp.dot(lhs, w0_ref[...], preferred_element_type=F32)
        if kind == "glu":
            a = a * _sigmoid(jnp.dot(lhs, w1_ref[...], preferred_element_type=F32))
        u_dst[...] = a
        if kind == "gate":
            v_dst[...] = jnp.dot(lhs, w1_ref[...], preferred_element_type=F32)

    def dot_chunk_static(r):
        dot_chunk(r, u_ref.at[HALO + r:HALO + r + CHUNK, :],
                  v_ref.at[r:r + CHUNK, :] if kind == "gate" else None)

    def conv_chunk(r):
        for rs in range(0, CHUNK, SUB_ROWS):
            for cs in range(0, bn, SUB_COLS):
                cols = slice(cs, cs + SUB_COLS)
                acc = None
                for back in range(width):
                    xs = u_ref[pl.ds(HALO + r + rs - SUBLANES * back, SUB_ROWS), cols]
                    tap = width - 1 - back
                    term = cw_ref[tap:tap + 1, cols] * xs
                    acc = term if acc is None else acc + term
                y = acc + cb_ref[:, cols]
                if kind == "silu":
                    y = _silu(y)
                elif kind == "gate":
                    y = _silu(y) * v_ref[pl.ds(r + rs, SUB_ROWS), cols]
                o_ref[pl.ds(r + rs, SUB_ROWS), cols] = y.astype(o_ref.dtype)

    dot_chunk_static((nch - 1) * CHUNK)
    tail = pltpu.roll(u_ref[HALO + seq - HALO:HALO + seq, :], 1, axis=0)
    first = lax.broadcasted_iota(jnp.int32, (HALO, bn), 0) % SUBLANES == 0
    u_ref[0:HALO, :] = jnp.where(first, 0.0, tail)
    dot_chunk_static(0)

    def step(i, carry):
        r = pl.multiple_of(i * CHUNK, CHUNK)
        dot_chunk(r + CHUNK, ustage_ref, vstage_ref)
        conv_chunk(r)
        u_ref[pl.ds(HALO + r + CHUNK, CHUNK), :] = ustage_ref[...]
        if kind == "gate":
            v_ref[pl.ds(r + CHUNK, CHUNK), :] = vstage_ref[...]
        return carry
    lax.fori_loop(0, nch - 2, step, 0)
    conv_chunk((nch - 2) * CHUNK)
    conv_chunk((nch - 1) * CHUNK)


def _seqmm_conv(h, ws, conv_w, conv_b, kind, seq, bn, name):
    m = h.shape[0]
    n = ws[0].shape[1]
    scratch = [pltpu.VMEM((HALO + seq, bn), F32), pltpu.VMEM((CHUNK, bn), F32)]
    if kind == "gate":
        scratch += [pltpu.VMEM((seq, bn), F32), pltpu.VMEM((CHUNK, bn), F32)]
    assert conv_w.shape[0] - 1 <= HALO // SUBLANES
    return pl.pallas_call(
        functools.partial(_seqmm_conv_body, seq, conv_w.shape[0], kind),
        grid=(m // seq, n // bn),
        in_specs=_seq_specs(h, ws, [conv_w, conv_b], seq, bn),
        out_specs=pl.BlockSpec((seq, bn), lambda b, j: (b, j)),
        out_shape=jax.ShapeDtypeStruct((m, n), BF16),
        scratch_shapes=scratch,
        compiler_params=_params("parallel", "arbitrary"),
        name=name,
    )(h, *ws, conv_w, conv_b)


def _ssd_body(x_ref, b_ref, c_ref, dt_ref, z_ref, alog_ref, dskip_ref, gn_ref, o_ref, state_ref):
    q = SSM_CHUNK
    p = SSM_HEAD_DIM
    n = SSM_STATE
    heads_per_group = x_ref.shape[1] // (SSM_GROUPS * p)
    gw = heads_per_group * p

    @pl.when(pl.program_id(1) == 0)
    def _():
        state_ref[...] = jnp.zeros_like(state_ref)

    dt = dt_ref[...]
    dta = dt * (-jnp.exp(alog_ref[...]))
    row = lax.broadcasted_iota(jnp.int32, (q, q), 0)
    col = lax.broadcasted_iota(jnp.int32, (q, q), 1)
    causal = row >= col
    a_cum = jnp.dot(causal.astype(F32), dta, precision=lax.Precision.HIGHEST,
                    preferred_element_type=F32)
    a_cum_t = a_cum.T
    a_last = a_cum[q - 1:q, :]
    e_cum = jnp.exp(a_cum)
    dt_end = dt * jnp.exp(a_last - a_cum)
    chunk_decay = jnp.exp(a_last)

    for g in range(SSM_GROUPS):
        bg = b_ref[:, g * n:(g + 1) * n]
        cg = c_ref[:, g * n:(g + 1) * n]
        cb = lax.dot_general(cg, bg, (((1,), (1,)), ((), ())), preferred_element_type=F32)
        prev = state_ref[g]
        c_prev = jnp.dot(cg, prev.astype(BF16), preferred_element_type=F32)
        ys, xends, decays = [], [], []
        for r in range(heads_per_group):
            hd = g * heads_per_group + r
            xh = x_ref[:, hd * p:(hd + 1) * p].astype(F32)
            seg = a_cum[:, hd:hd + 1] - a_cum_t[hd:hd + 1, :]
            decay = jnp.exp(jnp.where(causal, seg, -jnp.inf))
            y_diag = jnp.dot((decay * cb).astype(BF16), (xh * dt[:, hd:hd + 1]).astype(BF16),
                             preferred_element_type=F32)
            y_off = c_prev[:, r * p:(r + 1) * p] * e_cum[:, hd:hd + 1]
            ys.append(y_diag + y_off + xh * dskip_ref[:, hd * p:(hd + 1) * p])
            xends.append((xh * dt_end[:, hd:hd + 1]).astype(BF16))
            decays.append(jnp.broadcast_to(chunk_decay[:, hd:hd + 1], (1, p)))
        new = lax.dot_general(bg, jnp.concatenate(xends, axis=1), (((0,), (0,)), ((), ())),
                              preferred_element_type=F32)
        state_ref[g] = prev * jnp.concatenate(decays, axis=1) + new
        cols = slice(g * gw, (g + 1) * gw)
        yz = jnp.concatenate(ys, axis=1) * z_ref[:, cols].astype(F32)
        o_ref[:, cols] = (_rms_rows(yz) * gn_ref[:, cols]).astype(o_ref.dtype)


def _ssd(xbc, dt, zs, a_log_pad, dskip_cols, g_norm, seq):
    m, xbc_w = xbc.shape
    d_inner = zs.shape[1]
    bc_w = SSM_GROUPS * SSM_STATE
    nch = seq // CHUNK
    q = SSM_CHUNK
    assert q == CHUNK and nch == SUBLANES
    xbc_v = xbc.reshape(m // nch, nch * xbc_w)
    x_blk = xbc_w // d_inner
    bc_blk = xbc_w // bc_w
    out = pl.pallas_call(
        _ssd_body,
        grid=(m // seq, nch),
        in_specs=[pl.BlockSpec((q, d_inner), lambda b, s: (b, s * x_blk)),
                  pl.BlockSpec((q, bc_w), lambda b, s: (b, s * bc_blk + d_inner // bc_w)),
                  pl.BlockSpec((q, bc_w), lambda b, s: (b, s * bc_blk + d_inner // bc_w + 1)),
                  pl.BlockSpec((q, LANES), lambda b, s: (b, s)),
                  pl.BlockSpec((q, d_inner), lambda b, s: (b, s)),
                  pl.BlockSpec((1, LANES), lambda b, s: (0, 0)),
                  pl.BlockSpec((1, d_inner), lambda b, s: (0, 0)),
                  pl.BlockSpec((1, d_inner), lambda b, s: (0, 0))],
        out_specs=pl.BlockSpec((q, d_inner), lambda b, s: (b, s)),
        out_shape=jax.ShapeDtypeStruct((m // nch, nch * d_inner), BF16),
        scratch_shapes=[pltpu.VMEM((SSM_GROUPS, SSM_STATE, d_inner // SSM_GROUPS), F32)],
        compiler_params=_params("parallel", "arbitrary"),
        name="ssd",
    )(xbc_v, xbc_v, xbc_v, dt.reshape(m // nch, nch * LANES), zs.reshape(m // nch, nch * d_inner),
      a_log_pad, dskip_cols, g_norm)
    return out.reshape(m, d_inner)


def _merge_body(yn_ref, uc_ref, ga_ref, gb_ref, wa_ref, wb_ref, lng_ref, lnb_ref, bo_ref, o_ref):
    u = uc_ref[...].astype(F32)
    mu = jnp.mean(u, axis=-1, keepdims=True)
    ctr = u - mu
    var = jnp.mean(ctr * ctr, axis=-1, keepdims=True)
    un = _silu(ctr * lax.rsqrt(var + LN_EPS) * lng_ref[...] + lnb_ref[...]).astype(BF16)
    ya = jnp.dot(yn_ref[...], wa_ref[...], preferred_element_type=F32)
    yb = jnp.dot(un, wb_ref[...], preferred_element_type=F32) + bo_ref[...]
    o_ref[...] = (ga_ref[...].astype(F32) * ya + gb_ref[...].astype(F32) * yb).astype(BF16)


def _merge(yn, uc, gates, wa, wb, ln_g, ln_b, b_out, bm=512):
    m, d = yn.shape
    row = pl.BlockSpec((bm, d), lambda i: (i, 0))
    vec = pl.BlockSpec((1, d), lambda i: (0, 0))
    wspec = pl.BlockSpec((d, d), lambda i: (0, 0), pipeline_mode=pl.Buffered(1))
    return pl.pallas_call(
        _merge_body,
        grid=(m // bm,),
        in_specs=[row, row, row, pl.BlockSpec((bm, d), lambda i: (i, 1)), wspec, wspec,
                  vec, vec, vec],
        out_specs=row,
        out_shape=jax.ShapeDtypeStruct((m, d), BF16),
        compiler_params=_params("parallel"),
        name="merge",
    )(yn, uc, gates, gates, wa, wb, ln_g, ln_b, b_out)


def _oproj_body(mg_ref, x_ref, w_ref, g2_ref, mod_ref, x1_ref, h2_ref):
    gate = mod_ref[0, 2:3, :]
    shift = mod_ref[0, 3:4, :]
    scale = mod_ref[0, 4:5, :]
    x1 = x_ref[...] + gate * jnp.dot(mg_ref[...], w_ref[...], preferred_element_type=F32)
    x1_ref[...] = x1
    h2_ref[...] = (_rms_rows(x1) * g2_ref[...] * (1.0 + scale) + shift).astype(BF16)


def _oproj(merged, x2, w_o, g2, mod3, seq):
    m, d = x2.shape
    nch = seq // CHUNK
    nat = pl.BlockSpec((CHUNK, d), lambda b, s: (b * nch + s, 0))
    itl = pl.BlockSpec((CHUNK, d), lambda b, s: (b, s))
    x1, h2 = pl.pallas_call(
        _oproj_body,
        grid=(m // seq, nch),
        in_specs=[itl, nat,
                  pl.BlockSpec((d, d), lambda b, s: (0, 0), pipeline_mode=pl.Buffered(1)),
                  pl.BlockSpec((1, d), lambda b, s: (0, 0)),
                  pl.BlockSpec((1, N_MOD, d), lambda b, s: (b, 0, 0))],
        out_specs=[nat, itl],
        out_shape=[jax.ShapeDtypeStruct((m, d), F32),
                   jax.ShapeDtypeStruct((m // nch, nch * d), BF16)],
        compiler_params=_params("parallel", "parallel"),
        name="oproj",
    )(merged.reshape(m // nch, nch * d), x2, w_o, g2, mod3)
    return x1, h2.reshape(m, d)


def _down_body(g_ref, x1_ref, w_ref, gf_ref, mod_ref, o_ref):
    gate = mod_ref[0, 5:6, :]
    x2 = x1_ref[...] + gate * jnp.dot(g_ref[...], w_ref[...], preferred_element_type=F32)
    o_ref[...] = _rms_rows(x2) * gf_ref[...]


def _down(gff, x1, w_down, g_final, mod3, seq):
    m, d = x1.shape
    dff = gff.shape[1]
    nch = seq // CHUNK
    nat = pl.BlockSpec((CHUNK, d), lambda b, s: (b * nch + s, 0))
    return pl.pallas_call(
        _down_body,
        grid=(m // seq, nch),
        in_specs=[pl.BlockSpec((CHUNK, dff), lambda b, s: (b, s)), nat,
                  pl.BlockSpec((dff, d), lambda b, s: (0, 0), pipeline_mode=pl.Buffered(1)),
                  pl.BlockSpec((1, d), lambda b, s: (0, 0)),
                  pl.BlockSpec((1, N_MOD, d), lambda b, s: (b, 0, 0))],
        out_specs=nat,
        out_shape=jax.ShapeDtypeStruct((m, d), F32),
        compiler_params=_params("parallel", "parallel"),
        name="down",
    )(gff.reshape(m // nch, nch * dff), x1, w_down, g_final, mod3)


def _layer(x, c, w_ada, b_ada, g_norm1, w_in, w_ssm_conv, b_ssm_conv, dt_bias, a_log, d_skip,
           g_ssm_norm, w_ssm_out, w_cfm_dw, b_cfm_dw, g_cfm_ln, b_cfm_ln, w_cfm_out, b_cfm_out,
           w_o, g_norm2, w_up, w_ff_conv, b_ff_conv, w_down, g_final):
    bsz, seq, d = x.shape
    assert seq == SUBLANES * CHUNK
    d_inner = w_ssm_out.shape[0]
    heads = dt_bias.shape[0]
    xbc_dim = w_ssm_conv.shape[1]
    cfm_d = w_cfm_out.shape[0]
    d_ff = w_down.shape[0]
    row = lambda v: v.reshape(1, -1).astype(F32)

    o_xbc = d_inner
    o_dt = o_xbc + xbc_dim
    o_lin = o_dt + heads
    o_gate = o_lin + cfm_d
    o_g = o_gate + cfm_d
    wb = lambda lo, hi: w_in[:, lo:hi].astype(BF16)
    w_dt = jnp.pad(w_in[:, o_dt:o_lin], ((0, 0), (0, LANES - heads))).astype(BF16)
    dt_bias_pad = jnp.pad(row(dt_bias), ((0, 0), (0, LANES - heads)))
    a_log_pad = jnp.pad(row(a_log), ((0, 0), (0, LANES - heads)))
    dskip_cols = jnp.repeat(d_skip.astype(F32), SSM_HEAD_DIM).reshape(1, d_inner)

    x2 = x.reshape(bsz * seq, d)
    mod3 = _ada(c, w_ada, row(b_ada)).reshape(bsz, N_MOD, d)
    h = _norm1(x2, row(g_norm1), mod3, seq)

    zs = _seqmm_ew(h, wb(0, o_xbc), None, _silu, BF16, seq, 512, "in_z")
    xbc = _seqmm_conv(h, [wb(o_xbc, o_dt)], w_ssm_conv.astype(F32), row(b_ssm_conv), "silu",
                      seq, 512, "in_xbc")
    dt = _seqmm_ew(h, w_dt, dt_bias_pad, _softplus, F32, seq, LANES, "in_dt")
    uc = _seqmm_conv(h, [wb(o_lin, o_gate), wb(o_gate, o_g)], w_cfm_dw.astype(F32),
                     row(b_cfm_dw), "glu", seq, 512, "in_cfm")
    gates = _seqmm_ew(h, wb(o_g, w_in.shape[1]), None, _sigmoid, BF16, seq, 512, "in_gates")

    yn = _ssd(xbc, dt, zs, a_log_pad, dskip_cols, row(g_ssm_norm), seq)
    merged = _merge(yn, uc, gates, w_ssm_out.astype(BF16), w_cfm_out.astype(BF16),
                    row(g_cfm_ln), row(b_cfm_ln), row(b_cfm_out))
    x1, h2 = _oproj(merged, x2, w_o.astype(BF16), row(g_norm2), mod3, seq)

    gff = _seqmm_conv(h2, [w_up[:, :d_ff].astype(BF16), w_up[:, d_ff:].astype(BF16)],
                      w_ff_conv.astype(F32), row(b_ff_conv), "gate", seq, 512, "ffn_up")
    out = _down(gff, x1, w_down.astype(BF16), row(g_final), mod3, seq)
    return out.reshape(bsz, seq, d)


def kernel(x, c, w_ada, b_ada, g_norm1, w_in, w_ssm_conv, b_ssm_conv, dt_bias, a_log, d_skip, g_ssm_norm, w_ssm_out, w_cfm_dw, b_cfm_dw, g_cfm_ln, b_cfm_ln, w_cfm_out, b_cfm_out, w_o, g_norm2, w_up, w_ff_conv, b_ff_conv, w_down, g_final):
    assert w_ada.shape[0] == 1, "single-layer stack"
    return _layer(x, c, w_ada[0], b_ada[0], g_norm1[0], w_in[0], w_ssm_conv[0], b_ssm_conv[0],
                  dt_bias[0], a_log[0], d_skip[0], g_ssm_norm[0], w_ssm_out[0], w_cfm_dw[0],
                  b_cfm_dw[0], g_cfm_ln[0], b_cfm_ln[0], w_cfm_out[0], b_cfm_out[0], w_o[0],
                  g_norm2[0], w_up[0], w_ff_conv[0], b_ff_conv[0], w_down[0], g_final)
```

```python
import functools

import jax
import jax.numpy as jnp
from jax import lax
from jax.experimental import pallas as pl
from jax.experimental.pallas import tpu as pltpu

F32 = jnp.float32
BF16 = jnp.bfloat16

SSM_HEAD_DIM = 64
SSM_GROUPS = 8
SSM_STATE = 128
SSM_CHUNK = 256
N_MOD = 6
RMS_EPS = 1e-6
LN_EPS = 1e-5

LANES = 128
SUBLANES = 8
CHUNK = 256
SEG = CHUNK // SUBLANES
HALO = CHUNK
SUB_ROWS = 64
VMEM_LIMIT = 56 * 1024 * 1024


def _params(*sem):
    return pltpu.CompilerParams(dimension_semantics=sem, vmem_limit_bytes=VMEM_LIMIT)


def _sigmoid(v):
    return 1.0 / (1.0 + jnp.exp(-v))


def _silu(v):
    return v * _sigmoid(v)


def _softplus(v):
    return jnp.maximum(v, 0.0) + jnp.log1p(jnp.exp(-jnp.abs(v)))


def _rms_rows(v):
    return v * lax.rsqrt(jnp.mean(v * v, axis=-1, keepdims=True) + RMS_EPS)


def _ada_body(c_ref, w_ref, b_ref, o_ref):
    s = _silu(c_ref[...]).astype(BF16)
    o_ref[...] = jnp.dot(s, w_ref[...].astype(BF16), preferred_element_type=F32) + b_ref[...]


def _ada(c, w, b, bn=1024):
    bsz, d = c.shape
    n = w.shape[1]
    return pl.pallas_call(
        _ada_body,
        grid=(n // bn,),
        in_specs=[pl.BlockSpec((bsz, d), lambda j: (0, 0)),
                  pl.BlockSpec((d, bn), lambda j: (0, j)),
                  pl.BlockSpec((1, bn), lambda j: (0, j))],
        out_specs=pl.BlockSpec((bsz, bn), lambda j: (0, j)),
        out_shape=jax.ShapeDtypeStruct((bsz, n), F32),
        compiler_params=_params("arbitrary"),
        name="ada",
    )(c, w, b)


def _norm1_body(x_ref, g_ref, mod_ref, o_ref):
    shift = mod_ref[0, 0:1, :]
    scale = mod_ref[0, 1:2, :]
    y = _rms_rows(x_ref[...]) * g_ref[...]
    o_ref[...] = (y * (1.0 + scale) + shift).astype(BF16)


def _norm1(x2, g, mod3, seq, bm=512):
    m, d = x2.shape
    per = seq // bm
    return pl.pallas_call(
        _norm1_body,
        grid=(m // bm,),
        in_specs=[pl.BlockSpec((bm, d), lambda i: (i, 0)),
                  pl.BlockSpec((1, d), lambda i: (0, 0)),
                  pl.BlockSpec((1, N_MOD, d), lambda i: (i // per, 0, 0))],
        out_specs=pl.BlockSpec((bm, d), lambda i: (i, 0)),
        out_shape=jax.ShapeDtypeStruct((m, d), BF16),
        compiler_params=_params("parallel"),
        name="norm1",
    )(x2, g, mod3)


def _seq_specs(h, ws, consts, seq, bn):
    k = h.shape[1]
    in_specs = [pl.BlockSpec((seq, k), lambda b, j: (b, 0))]
    in_specs += [pl.BlockSpec((k, bn), lambda b, j: (0, j)) for _ in ws]
    in_specs += [pl.BlockSpec((cst.shape[0], bn), lambda b, j: (0, j)) for cst in consts]
    return in_specs


def _seqmm_ew_body(seq, fn, has_bias, h_ref, w_ref, *refs):
    o_ref = refs[-1]
    for i in range(seq // CHUNK):
        rows = slice(i * CHUNK, (i + 1) * CHUNK)
        a = jnp.dot(h_ref[rows, :], w_ref[...], preferred_element_type=F32)
        if has_bias:
            a = a + refs[0][...]
        o_ref[rows, :] = fn(a).astype(o_ref.dtype)


def _seqmm_ew(h, w, bias, fn, out_dtype, seq, bn, name):
    m = h.shape[0]
    n = w.shape[1]
    consts = [] if bias is None else [bias]
    return pl.pallas_call(
        functools.partial(_seqmm_ew_body, seq, fn, bias is not None),
        grid=(m // seq, n // bn),
        in_specs=_seq_specs(h, [w], consts, seq, bn),
        out_specs=pl.BlockSpec((seq, bn), lambda b, j: (b, j)),
        out_shape=jax.ShapeDtypeStruct((m, n), out_dtype),
        compiler_params=_params("parallel", "arbitrary"),
        name=name,
    )(h, w, *consts)


def _seqmm_conv_body(seq, width, kind, h_ref, w0_ref, *refs):
    two = kind in ("glu", "gate")
    w1_ref = refs[0] if two else None
    cw_ref, cb_ref, o_ref, u_ref, y_ref = refs[two:two + 5]
    v_ref = refs[two + 5] if kind == "gate" else None
    nch = seq // CHUNK
    ncol = o_ref.shape[1] // LANES

    def dot_chunk(i):
        lhs = jnp.concatenate(
            [h_ref[s * CHUNK + i * SEG:s * CHUNK + (i + 1) * SEG, :] for s in range(SUBLANES)],
            axis=0)
        a = jnp.dot(lhs, w0_ref[...], preferred_element_type=F32)
        if kind == "glu":
            a = a * _sigmoid(jnp.dot(lhs, w1_ref[...], preferred_element_type=F32))
        for c in range(ncol):
            for s in range(SUBLANES):
                u_ref[c, pl.ds(HALO + i * CHUNK + s, SEG, stride=SUBLANES), :] = (
                    a[s * SEG:(s + 1) * SEG, c * LANES:(c + 1) * LANES])
        if kind == "gate":
            v_ref[i * CHUNK:(i + 1) * CHUNK, :] = jnp.dot(lhs, w1_ref[...],
                                                         preferred_element_type=F32)

    def conv_chunk(i):
        r = i * CHUNK
        for c in range(ncol):
            cols = slice(c * LANES, (c + 1) * LANES)
            for rs in range(0, CHUNK, SUB_ROWS):
                acc = None
                for back in range(width):
                    lo = HALO + r + rs - SUBLANES * back
                    tap = width - 1 - back
                    term = cw_ref[tap:tap + 1, cols] * u_ref[c, lo:lo + SUB_ROWS, :]
                    acc = term if acc is None else acc + term
                y_ref[c, r + rs:r + rs + SUB_ROWS, :] = acc + cb_ref[:, cols]
            for s in range(SUBLANES):
                y = y_ref[c, pl.ds(r + s, SEG, stride=SUBLANES), :]
                if kind == "silu":
                    y = _silu(y)
                elif kind == "gate":
                    y = _silu(y) * v_ref[r + s * SEG:r + (s + 1) * SEG, cols]
                o_ref[s * CHUNK + i * SEG:s * CHUNK + (i + 1) * SEG, cols] = y.astype(o_ref.dtype)

    dot_chunk(nch - 1)
    first = lax.broadcasted_iota(jnp.int32, (HALO, LANES), 0) % SUBLANES == 0
    for c in range(ncol):
        tail = pltpu.roll(u_ref[c, seq:HALO + seq, :], 1, axis=0)
        u_ref[c, 0:HALO, :] = jnp.where(first, 0.0, tail)
    dot_chunk(0)
    for i in range(nch - 2):
        dot_chunk(i + 1)
        conv_chunk(i)
    conv_chunk(nch - 2)
    conv_chunk(nch - 1)


def _seqmm_conv(h, ws, conv_w, conv_b, kind, seq, bn, name):
    m = h.shape[0]
    n = ws[0].shape[1]
    assert seq == SUBLANES * CHUNK and conv_w.shape[0] - 1 <= HALO // SUBLANES
    scratch = [pltpu.VMEM((bn // LANES, HALO + seq, LANES), F32),
               pltpu.VMEM((bn // LANES, seq, LANES), F32)]
    if kind == "gate":
        scratch.append(pltpu.VMEM((seq, bn), F32))
    return pl.pallas_call(
        functools.partial(_seqmm_conv_body, seq, conv_w.shape[0], kind),
        grid=(m // seq, n // bn),
        in_specs=_seq_specs(h, ws, [conv_w, conv_b], seq, bn),
        out_specs=pl.BlockSpec((seq, bn), lambda b, j: (b, j)),
        out_shape=jax.ShapeDtypeStruct((m, n), BF16),
        scratch_shapes=scratch,
        compiler_params=_params("parallel", "arbitrary"),
        name=name,
    )(h, *ws, conv_w, conv_b)


def _ssd_body(x_ref, b_ref, c_ref, dt_ref, z_ref, alog_ref, dskip_ref, gn_ref, o_ref, state_ref):
    q = SSM_CHUNK
    p = SSM_HEAD_DIM
    n = SSM_STATE
    heads_per_group = x_ref.shape[1] // (SSM_GROUPS * p)
    gw = heads_per_group * p

    @pl.when(pl.program_id(1) == 0)
    def _():
        state_ref[...] = jnp.zeros_like(state_ref)

    dt = dt_ref[...]
    dta = dt * (-jnp.exp(alog_ref[...]))
    row = lax.broadcasted_iota(jnp.int32, (q, q), 0)
    col = lax.broadcasted_iota(jnp.int32, (q, q), 1)
    causal = row >= col
    a_cum = jnp.dot(causal.astype(F32), dta, precision=lax.Precision.HIGHEST,
                    preferred_element_type=F32)
    a_cum_t = a_cum.T
    a_last = a_cum[q - 1:q, :]
    e_cum = jnp.exp(a_cum)
    dt_end = dt * jnp.exp(a_last - a_cum)
    chunk_decay = jnp.exp(a_last)

    for g in range(SSM_GROUPS):
        bg = b_ref[:, g * n:(g + 1) * n]
        cg = c_ref[:, g * n:(g + 1) * n]
        cb = lax.dot_general(cg, bg, (((1,), (1,)), ((), ())), preferred_element_type=F32)
        prev = state_ref[g]
        c_prev = jnp.dot(cg, prev.astype(BF16), preferred_element_type=F32)
        ys, xends, decays = [], [], []
        for r in range(heads_per_group):
            hd = g * heads_per_group + r
            xh = x_ref[:, hd * p:(hd + 1) * p].astype(F32)
            seg = a_cum[:, hd:hd + 1] - a_cum_t[hd:hd + 1, :]
            decay = jnp.exp(jnp.where(causal, seg, -jnp.inf))
            y_diag = jnp.dot((decay * cb).astype(BF16), (xh * dt[:, hd:hd + 1]).astype(BF16),
                             preferred_element_type=F32)
            y_off = c_prev[:, r * p:(r + 1) * p] * e_cum[:, hd:hd + 1]
            ys.append(y_diag + y_off + xh * dskip_ref[:, hd * p:(hd + 1) * p])
            xends.append((xh * dt_end[:, hd:hd + 1]).astype(BF16))
            decays.append(jnp.broadcast_to(chunk_decay[:, hd:hd + 1], (1, p)))
        new = lax.dot_general(bg, jnp.concatenate(xends, axis=1), (((0,), (0,)), ((), ())),
                              preferred_element_type=F32)
        state_ref[g] = prev * jnp.concatenate(decays, axis=1) + new
        cols = slice(g * gw, (g + 1) * gw)
        yz = jnp.concatenate(ys, axis=1) * z_ref[:, cols].astype(F32)
        o_ref[:, cols] = (_rms_rows(yz) * gn_ref[:, cols]).astype(o_ref.dtype)


def _ssd(xbc, dt, zs, a_log_pad, dskip_cols, g_norm, seq):
    m = xbc.shape[0]
    d_inner = zs.shape[1]
    bc_w = SSM_GROUPS * SSM_STATE
    nchunk = seq // SSM_CHUNK
    q = SSM_CHUNK
    rows = lambda b, c: b * nchunk + c
    return pl.pallas_call(
        _ssd_body,
        grid=(m // seq, nchunk),
        in_specs=[pl.BlockSpec((q, d_inner), lambda b, c: (rows(b, c), 0)),
                  pl.BlockSpec((q, bc_w), lambda b, c: (rows(b, c), d_inner // bc_w)),
                  pl.BlockSpec((q, bc_w), lambda b, c: (rows(b, c), d_inner // bc_w + 1)),
                  pl.BlockSpec((q, LANES), lambda b, c: (rows(b, c), 0)),
                  pl.BlockSpec((q, d_inner), lambda b, c: (rows(b, c), 0)),
                  pl.BlockSpec((1, LANES), lambda b, c: (0, 0)),
                  pl.BlockSpec((1, d_inner), lambda b, c: (0, 0)),
                  pl.BlockSpec((1, d_inner), lambda b, c: (0, 0))],
        out_specs=pl.BlockSpec((q, d_inner), lambda b, c: (rows(b, c), 0)),
        out_shape=jax.ShapeDtypeStruct((m, d_inner), BF16),
        scratch_shapes=[pltpu.VMEM((SSM_GROUPS, SSM_STATE, d_inner // SSM_GROUPS), F32)],
        compiler_params=_params("parallel", "arbitrary"),
        name="ssd",
    )(xbc, xbc, xbc, dt, zs, a_log_pad, dskip_cols, g_norm)


def _merge_body(yn_ref, uc_ref, ga_ref, gb_ref, wa_ref, wb_ref, lng_ref, lnb_ref, bo_ref, o_ref):
    u = uc_ref[...].astype(F32)
    mu = jnp.mean(u, axis=-1, keepdims=True)
    ctr = u - mu
    var = jnp.mean(ctr * ctr, axis=-1, keepdims=True)
    un = _silu(ctr * lax.rsqrt(var + LN_EPS) * lng_ref[...] + lnb_ref[...]).astype(BF16)
    ya = jnp.dot(yn_ref[...], wa_ref[...], preferred_element_type=F32)
    yb = jnp.dot(un, wb_ref[...], preferred_element_type=F32) + bo_ref[...]
    o_ref[...] = (ga_ref[...].astype(F32) * ya + gb_ref[...].astype(F32) * yb).astype(BF16)


def _merge(yn, uc, gates, wa, wb, ln_g, ln_b, b_out, bm=512):
    m, d = yn.shape
    row = pl.BlockSpec((bm, d), lambda i: (i, 0))
    vec = pl.BlockSpec((1, d), lambda i: (0, 0))
    wspec = pl.BlockSpec((d, d), lambda i: (0, 0), pipeline_mode=pl.Buffered(1))
    return pl.pallas_call(
        _merge_body,
        grid=(m // bm,),
        in_specs=[row, row, row, pl.BlockSpec((bm, d), lambda i: (i, 1)), wspec, wspec,
                  vec, vec, vec],
        out_specs=row,
        out_shape=jax.ShapeDtypeStruct((m, d), BF16),
        compiler_params=_params("parallel"),
        name="merge",
    )(yn, uc, gates, gates, wa, wb, ln_g, ln_b, b_out)


def _oproj_body(mg_ref, x_ref, w_ref, g2_ref, mod_ref, x1_ref, h2_ref):
    gate = mod_ref[0, 2:3, :]
    shift = mod_ref[0, 3:4, :]
    scale = mod_ref[0, 4:5, :]
    x1 = x_ref[...] + gate * jnp.dot(mg_ref[...], w_ref[...], preferred_element_type=F32)
    x1_ref[...] = x1
    h2_ref[...] = (_rms_rows(x1) * g2_ref[...] * (1.0 + scale) + shift).astype(BF16)


def _oproj(merged, x2, w_o, g2, mod3, seq, bm=512):
    m, d = x2.shape
    per = seq // bm
    row = pl.BlockSpec((bm, d), lambda i: (i, 0))
    return pl.pallas_call(
        _oproj_body,
        grid=(m // bm,),
        in_specs=[row, row,
                  pl.BlockSpec((d, d), lambda i: (0, 0), pipeline_mode=pl.Buffered(1)),
                  pl.BlockSpec((1, d), lambda i: (0, 0)),
                  pl.BlockSpec((1, N_MOD, d), lambda i: (i // per, 0, 0))],
        out_specs=[row, row],
        out_shape=[jax.ShapeDtypeStruct((m, d), F32), jax.ShapeDtypeStruct((m, d), BF16)],
        compiler_params=_params("parallel"),
        name="oproj",
    )(merged, x2, w_o, g2, mod3)


def _down_body(g_ref, x1_ref, w_ref, gf_ref, mod_ref, o_ref):
    gate = mod_ref[0, 5:6, :]
    x2 = x1_ref[...] + gate * jnp.dot(g_ref[...], w_ref[...], preferred_element_type=F32)
    o_ref[...] = _rms_rows(x2) * gf_ref[...]


def _down(gff, x1, w_down, g_final, mod3, seq, bm=256):
    m, d = x1.shape
    dff = gff.shape[1]
    per = seq // bm
    row = pl.BlockSpec((bm, d), lambda i: (i, 0))
    return pl.pallas_call(
        _down_body,
        grid=(m // bm,),
        in_specs=[pl.BlockSpec((bm, dff), lambda i: (i, 0)), row,
                  pl.BlockSpec((dff, d), lambda i: (0, 0), pipeline_mode=pl.Buffered(1)),
                  pl.BlockSpec((1, d), lambda i: (0, 0)),
                  pl.BlockSpec((1, N_MOD, d), lambda i: (i // per, 0, 0))],
        out_specs=row,
        out_shape=jax.ShapeDtypeStruct((m, d), F32),
        compiler_params=_params("parallel"),
        name="down",
    )(gff, x1, w_down, g_final, mod3)


def _layer(x, c, w_ada, b_ada, g_norm1, w_in, w_ssm_conv, b_ssm_conv, dt_bias, a_log, d_skip,
           g_ssm_norm, w_ssm_out, w_cfm_dw, b_cfm_dw, g_cfm_ln, b_cfm_ln, w_cfm_out, b_cfm_out,
           w_o, g_norm2, w_up, w_ff_conv, b_ff_conv, w_down, g_final):
    bsz, seq, d = x.shape
    d_inner = w_ssm_out.shape[0]
    heads = dt_bias.shape[0]
    xbc_dim = w_ssm_conv.shape[1]
    cfm_d = w_cfm_out.shape[0]
    d_ff = w_down.shape[0]
    row = lambda v: v.reshape(1, -1).astype(F32)

    o_xbc = d_inner
    o_dt = o_xbc + xbc_dim
    o_lin = o_dt + heads
    o_gate = o_lin + cfm_d
    o_g = o_gate + cfm_d
    wb = lambda lo, hi: w_in[:, lo:hi].astype(BF16)
    w_dt = jnp.pad(w_in[:, o_dt:o_lin], ((0, 0), (0, LANES - heads))).astype(BF16)
    dt_bias_pad = jnp.pad(row(dt_bias), ((0, 0), (0, LANES - heads)))
    a_log_pad = jnp.pad(row(a_log), ((0, 0), (0, LANES - heads)))
    dskip_cols = jnp.repeat(d_skip.astype(F32), SSM_HEAD_DIM).reshape(1, d_inner)

    x2 = x.reshape(bsz * seq, d)
    mod3 = _ada(c, w_ada, row(b_ada)).reshape(bsz, N_MOD, d)
    h = _norm1(x2, row(g_norm1), mod3, seq)

    zs = _seqmm_ew(h, wb(0, o_xbc), None, _silu, BF16, seq, 512, "in_z")
    xbc = _seqmm_conv(h, [wb(o_xbc, o_dt)], w_ssm_conv.astype(F32), row(b_ssm_conv), "silu",
                      seq, 512, "in_xbc")
    dt = _seqmm_ew(h, w_dt, dt_bias_pad, _softplus, F32, seq, LANES, "in_dt")
    uc = _seqmm_conv(h, [wb(o_lin, o_gate), wb(o_gate, o_g)], w_cfm_dw.astype(F32),
                     row(b_cfm_dw), "glu", seq, 512, "in_cfm")
    gates = _seqmm_ew(h, wb(o_g, w_in.shape[1]), None, _sigmoid, BF16, seq, 512, "in_gates")

    yn = _ssd(xbc, dt, zs, a_log_pad, dskip_cols, row(g_ssm_norm), seq)
    merged = _merge(yn, uc, gates, w_ssm_out.astype(BF16), w_cfm_out.astype(BF16),
                    row(g_cfm_ln), row(b_cfm_ln), row(b_cfm_out))
    x1, h2 = _oproj(merged, x2, w_o.astype(BF16), row(g_norm2), mod3, seq)

    gff = _seqmm_conv(h2, [w_up[:, :d_ff].astype(BF16), w_up[:, d_ff:].astype(BF16)],
                      w_ff_conv.astype(F32), row(b_ff_conv), "gate", seq, 512, "ffn_up")
    out = _down(gff, x1, w_down.astype(BF16), row(g_final), mod3, seq)
    return out.reshape(bsz, seq, d)


def kernel(x, c, w_ada, b_ada, g_norm1, w_in, w_ssm_conv, b_ssm_conv, dt_bias, a_log, d_skip, g_ssm_norm, w_ssm_out, w_cfm_dw, b_cfm_dw, g_cfm_ln, b_cfm_ln, w_cfm_out, b_cfm_out, w_o, g_norm2, w_up, w_ff_conv, b_ff_conv, w_down, g_final):
    assert w_ada.shape[0] == 1, "single-layer stack"
    return _layer(x, c, w_ada[0], b_ada[0], g_norm1[0], w_in[0], w_ssm_conv[0], b_ssm_conv[0],
                  dt_bias[0], a_log[0], d_skip[0], g_ssm_norm[0], w_ssm_out[0], w_cfm_dw[0],
                  b_cfm_dw[0], g_cfm_ln[0], b_cfm_ln[0], w_cfm_out[0], b_cfm_out[0], w_o[0],
                  g_norm2[0], w_up[0], w_ff_conv[0], b_ff_conv[0], w_down[0], g_final)
```

```python
import functools
from typing import NamedTuple

import jax
import jax.numpy as jnp
from jax import lax
from jax.experimental import pallas as pl
from jax.experimental.pallas import tpu as pltpu

F32 = jnp.float32
BF16 = jnp.bfloat16

SSM_HEAD_DIM = 64
SSM_GROUPS = 8
SSM_STATE = 128
SSM_CHUNK = 256
N_MOD = 6
RMS_EPS = 1e-6
LN_EPS = 1e-5

LANES = 128
SUBLANES = 8
CHUNK = 256
SEG = CHUNK // SUBLANES
HALO = CHUNK
SUB_ROWS = 64
VMEM_LIMIT = 56 * 1024 * 1024


def _params(*sem):
    return pltpu.CompilerParams(dimension_semantics=sem, vmem_limit_bytes=VMEM_LIMIT)


def _sigmoid(v):
    return 1.0 / (1.0 + jnp.exp(-v))


def _silu(v):
    return v * _sigmoid(v)


def _softplus(v):
    return jnp.maximum(v, 0.0) + jnp.log1p(jnp.exp(-jnp.abs(v)))


def _rms_rows(v):
    return v * lax.rsqrt(jnp.mean(v * v, axis=-1, keepdims=True) + RMS_EPS)


def _ada_body(c_ref, w_ref, b_ref, o_ref):
    s = _silu(c_ref[...]).astype(BF16)
    o_ref[...] = jnp.dot(s, w_ref[...].astype(BF16), preferred_element_type=F32) + b_ref[...]


def _ada(c, w, b, bn=1024):
    bsz, d = c.shape
    n = w.shape[1]
    return pl.pallas_call(
        _ada_body,
        grid=(n // bn,),
        in_specs=[pl.BlockSpec((bsz, d), lambda j: (0, 0)),
                  pl.BlockSpec((d, bn), lambda j: (0, j)),
                  pl.BlockSpec((1, bn), lambda j: (0, j))],
        out_specs=pl.BlockSpec((bsz, bn), lambda j: (0, j)),
        out_shape=jax.ShapeDtypeStruct((bsz, n), F32),
        compiler_params=_params("arbitrary"),
        name="ada",
    )(c, w, b)


def _norm1_body(x_ref, g_ref, mod_ref, o_ref):
    shift = mod_ref[0, 0:1, :]
    scale = mod_ref[0, 1:2, :]
    y = _rms_rows(x_ref[...]) * g_ref[...]
    o_ref[...] = (y * (1.0 + scale) + shift).astype(BF16)


def _norm1(x2, g, mod3, seq, bm=512):
    m, d = x2.shape
    per = seq // bm
    return pl.pallas_call(
        _norm1_body,
        grid=(m // bm,),
        in_specs=[pl.BlockSpec((bm, d), lambda i: (i, 0)),
                  pl.BlockSpec((1, d), lambda i: (0, 0)),
                  pl.BlockSpec((1, N_MOD, d), lambda i: (i // per, 0, 0))],
        out_specs=pl.BlockSpec((bm, d), lambda i: (i, 0)),
        out_shape=jax.ShapeDtypeStruct((m, d), BF16),
        compiler_params=_params("parallel"),
        name="norm1",
    )(x2, g, mod3)


class Cols(NamedTuple):
    arr: jax.Array
    start: int
    width: int


def _seq_specs(h, ws, consts, seq, bn):
    k = h.shape[1]
    in_specs = [pl.BlockSpec((seq, k), lambda b, j: (b, 0))]
    for w in ws:
        first, rem = divmod(w.start, bn)
        assert rem == 0 and w.width % bn == 0
        in_specs.append(pl.BlockSpec((k, bn), lambda b, j, first=first: (0, first + j)))
    in_specs += [pl.BlockSpec((cst.shape[0], bn), lambda b, j: (0, j)) for cst in consts]
    return in_specs


def _seqmm_ew_body(seq, fn, has_bias, h_ref, w_ref, *refs):
    o_ref = refs[-1]
    for i in range(seq // CHUNK):
        rows = slice(i * CHUNK, (i + 1) * CHUNK)
        a = jnp.dot(h_ref[rows, :], w_ref[...], preferred_element_type=F32)
        if has_bias:
            a = a + refs[0][...]
        o_ref[rows, :] = fn(a).astype(o_ref.dtype)


def _seqmm_ew(h, w, bias, fn, out_dtype, seq, bn, name):
    m = h.shape[0]
    n = w.width
    consts = [] if bias is None else [bias]
    return pl.pallas_call(
        functools.partial(_seqmm_ew_body, seq, fn, bias is not None),
        grid=(m // seq, n // bn),
        in_specs=_seq_specs(h, [w], consts, seq, bn),
        out_specs=pl.BlockSpec((seq, bn), lambda b, j: (b, j)),
        out_shape=jax.ShapeDtypeStruct((m, n), out_dtype),
        compiler_params=_params("parallel", "arbitrary"),
        name=name,
    )(h, w.arr, *consts)


def _seqmm_conv_body(seq, width, kind, h_ref, w0_ref, *refs):
    two = kind in ("glu", "gate")
    w1_ref = refs[0] if two else None
    cw_ref, cb_ref, o_ref, u_ref, y_ref = refs[two:two + 5]
    v_ref = refs[two + 5] if kind == "gate" else None
    nch = seq // CHUNK
    ncol = o_ref.shape[1] // LANES

    def dot_chunk(i):
        lhs = jnp.concatenate(
            [h_ref[s * CHUNK + i * SEG:s * CHUNK + (i + 1) * SEG, :] for s in range(SUBLANES)],
            axis=0)
        a = jnp.dot(lhs, w0_ref[...], preferred_element_type=F32)
        if kind == "glu":
            a = a * _sigmoid(jnp.dot(lhs, w1_ref[...], preferred_element_type=F32))
        for c in range(ncol):
            for s in range(SUBLANES):
                u_ref[c, pl.ds(HALO + i * CHUNK + s, SEG, stride=SUBLANES), :] = (
                    a[s * SEG:(s + 1) * SEG, c * LANES:(c + 1) * LANES])
        if kind == "gate":
            v_ref[i * CHUNK:(i + 1) * CHUNK, :] = jnp.dot(lhs, w1_ref[...],
                                                         preferred_element_type=F32)

    def conv_chunk(i):
        r = i * CHUNK
        for c in range(ncol):
            cols = slice(c * LANES, (c + 1) * LANES)
            for rs in range(0, CHUNK, SUB_ROWS):
                acc = None
                for back in range(width):
                    lo = HALO + r + rs - SUBLANES * back
                    tap = width - 1 - back
                    term = cw_ref[tap:tap + 1, cols] * u_ref[c, lo:lo + SUB_ROWS, :]
                    acc = term if acc is None else acc + term
                y_ref[c, r + rs:r + rs + SUB_ROWS, :] = acc + cb_ref[:, cols]
            for s in range(SUBLANES):
                y = y_ref[c, pl.ds(r + s, SEG, stride=SUBLANES), :]
                if kind == "silu":
                    y = _silu(y)
                elif kind == "gate":
                    y = _silu(y) * v_ref[r + s * SEG:r + (s + 1) * SEG, cols]
                o_ref[s * CHUNK + i * SEG:s * CHUNK + (i + 1) * SEG, cols] = y.astype(o_ref.dtype)

    dot_chunk(nch - 1)
    first = lax.broadcasted_iota(jnp.int32, (HALO, LANES), 0) % SUBLANES == 0
    for c in range(ncol):
        tail = pltpu.roll(u_ref[c, seq:HALO + seq, :], 1, axis=0)
        u_ref[c, 0:HALO, :] = jnp.where(first, 0.0, tail)
    dot_chunk(0)
    for i in range(nch - 2):
        dot_chunk(i + 1)
        conv_chunk(i)
    conv_chunk(nch - 2)
    conv_chunk(nch - 1)


def _seqmm_conv(h, ws, conv_w, conv_b, kind, seq, bn, name):
    m = h.shape[0]
    n = ws[0].width
    assert seq == SUBLANES * CHUNK and conv_w.shape[0] - 1 <= HALO // SUBLANES
    scratch = [pltpu.VMEM((bn // LANES, HALO + seq, LANES), F32),
               pltpu.VMEM((bn // LANES, seq, LANES), F32)]
    if kind == "gate":
        scratch.append(pltpu.VMEM((seq, bn), F32))
    return pl.pallas_call(
        functools.partial(_seqmm_conv_body, seq, conv_w.shape[0], kind),
        grid=(m // seq, n // bn),
        in_specs=_seq_specs(h, ws, [conv_w, conv_b], seq, bn),
        out_specs=pl.BlockSpec((seq, bn), lambda b, j: (b, j)),
        out_shape=jax.ShapeDtypeStruct((m, n), BF16),
        scratch_shapes=scratch,
        compiler_params=_params("parallel", "arbitrary"),
        name=name,
    )(h, *[w.arr for w in ws], conv_w, conv_b)


def _ssd_body(x_ref, b_ref, c_ref, dt_ref, z_ref, alog_ref, dskip_ref, gn_ref, e_ref, o_ref,
              state_ref, ex_ref):
    q = SSM_CHUNK
    p = SSM_HEAD_DIM
    n = SSM_STATE
    heads_per_group = x_ref.shape[1] // (SSM_GROUPS * p)
    gw = heads_per_group * p

    @pl.when(pl.program_id(1) == 0)
    def _():
        state_ref[...] = jnp.zeros_like(state_ref)

    dt = dt_ref[...]
    dta = dt * (-jnp.exp(alog_ref[...]))
    row = lax.broadcasted_iota(jnp.int32, (q, q), 0)
    col = lax.broadcasted_iota(jnp.int32, (q, q), 1)
    causal = row >= col
    a_cum = jnp.dot(causal.astype(F32), dta, precision=lax.Precision.HIGHEST,
                    preferred_element_type=F32)
    a_cum_t = a_cum.T
    a_last = a_cum[q - 1:q, :]

    per_head = jnp.concatenate(
        [dt, jnp.exp(a_cum), dt * jnp.exp(a_last - a_cum),
         jnp.broadcast_to(jnp.exp(a_last), (SUBLANES, LANES))], axis=0)
    hi = per_head.astype(BF16)
    lo = (per_head - hi.astype(F32)).astype(BF16)
    ex_ref[...] = (jnp.dot(hi, e_ref[...], preferred_element_type=F32)
                   + jnp.dot(lo, e_ref[...], preferred_element_type=F32))

    lane_head = lax.broadcasted_iota(jnp.int32, (1, gw), 1) // p
    for g in range(SSM_GROUPS):
        cols = slice(g * gw, (g + 1) * gw)
        bg = b_ref[:, g * n:(g + 1) * n]
        cg = c_ref[:, g * n:(g + 1) * n]
        cb = lax.dot_general(cg, bg, (((1,), (1,)), ((), ())), preferred_element_type=F32)
        prev = state_ref[g]
        xg = x_ref[:, cols].astype(F32)
        xdt = xg * ex_ref[0:q, cols]
        lhs, rhs = [], []
        for r in range(heads_per_group):
            hd = g * heads_per_group + r
            seg = a_cum[:, hd:hd + 1] - a_cum_t[hd:hd + 1, :]
            decay = jnp.exp(jnp.where(causal, seg, -jnp.inf))
            lhs.append((decay * cb).astype(BF16))
            rhs.append(jnp.where(lane_head == r, xdt, 0.0).astype(BF16))
        y = jnp.dot(jnp.concatenate(lhs, axis=1), jnp.concatenate(rhs, axis=0),
                    preferred_element_type=F32)
        y = y + jnp.dot(cg, prev.astype(BF16), preferred_element_type=F32) * ex_ref[q:2 * q, cols]
        y = y + xg * dskip_ref[:, cols]
        xend = (xg * ex_ref[2 * q:3 * q, cols]).astype(BF16)
        new = lax.dot_general(bg, xend, (((0,), (0,)), ((), ())), preferred_element_type=F32)
        state_ref[g] = prev * ex_ref[3 * q:3 * q + 1, cols] + new
        yz = y * z_ref[:, cols].astype(F32)
        o_ref[:, cols] = (_rms_rows(yz) * gn_ref[:, cols]).astype(o_ref.dtype)


def _ssd(xbc, dt, zs, a_log_pad, dskip_cols, g_norm, seq):
    m = xbc.shape[0]
    d_inner = zs.shape[1]
    bc_w = SSM_GROUPS * SSM_STATE
    nchunk = seq // SSM_CHUNK
    q = SSM_CHUNK
    rows = lambda b, c: b * nchunk + c
    spread = (jnp.arange(d_inner)[None, :] // SSM_HEAD_DIM == jnp.arange(LANES)[:, None]).astype(BF16)
    return pl.pallas_call(
        _ssd_body,
        grid=(m // seq, nchunk),
        in_specs=[pl.BlockSpec((q, d_inner), lambda b, c: (rows(b, c), 0)),
                  pl.BlockSpec((q, bc_w), lambda b, c: (rows(b, c), d_inner // bc_w)),
                  pl.BlockSpec((q, bc_w), lambda b, c: (rows(b, c), d_inner // bc_w + 1)),
                  pl.BlockSpec((q, LANES), lambda b, c: (rows(b, c), 0)),
                  pl.BlockSpec((q, d_inner), lambda b, c: (rows(b, c), 0)),
                  pl.BlockSpec((1, LANES), lambda b, c: (0, 0)),
                  pl.BlockSpec((1, d_inner), lambda b, c: (0, 0)),
                  pl.BlockSpec((1, d_inner), lambda b, c: (0, 0)),
                  pl.BlockSpec((LANES, d_inner), lambda b, c: (0, 0))],
        out_specs=pl.BlockSpec((q, d_inner), lambda b, c: (rows(b, c), 0)),
        out_shape=jax.ShapeDtypeStruct((m, d_inner), BF16),
        scratch_shapes=[pltpu.VMEM((SSM_GROUPS, SSM_STATE, d_inner // SSM_GROUPS), F32),
                        pltpu.VMEM((3 * q + SUBLANES, d_inner), F32)],
        compiler_params=_params("parallel", "arbitrary"),
        name="ssd",
    )(xbc, xbc, xbc, dt, zs, a_log_pad, dskip_cols, g_norm, spread)


def _merge_body(yn_ref, uc_ref, ga_ref, gb_ref, wa_ref, wb_ref, lng_ref, lnb_ref, bo_ref, o_ref):
    for r0 in range(0, o_ref.shape[0], CHUNK):
        rows = slice(r0, r0 + CHUNK)
        u = uc_ref[rows, :].astype(F32)
        mu = jnp.mean(u, axis=-1, keepdims=True)
        ctr = u - mu
        var = jnp.mean(ctr * ctr, axis=-1, keepdims=True)
        un = _silu(ctr * lax.rsqrt(var + LN_EPS) * lng_ref[...] + lnb_ref[...]).astype(BF16)
        ya = jnp.dot(yn_ref[rows, :], wa_ref[...], preferred_element_type=F32)
        yb = jnp.dot(un, wb_ref[...], preferred_element_type=F32) + bo_ref[...]
        o_ref[rows, :] = (ga_ref[rows, :].astype(F32) * ya
                          + gb_ref[rows, :].astype(F32) * yb).astype(BF16)


def _merge(yn, uc, gates, wa, wb, ln_g, ln_b, b_out, bm=512):
    m, d = yn.shape
    row = pl.BlockSpec((bm, d), lambda i: (i, 0))
    vec = pl.BlockSpec((1, d), lambda i: (0, 0))
    wspec = pl.BlockSpec((d, d), lambda i: (0, 0), pipeline_mode=pl.Buffered(1))
    return pl.pallas_call(
        _merge_body,
        grid=(m // bm,),
        in_specs=[row, row, row, pl.BlockSpec((bm, d), lambda i: (i, 1)), wspec, wspec,
                  vec, vec, vec],
        out_specs=row,
        out_shape=jax.ShapeDtypeStruct((m, d), BF16),
        compiler_params=_params("parallel"),
        name="merge",
    )(yn, uc, gates, gates, wa, wb, ln_g, ln_b, b_out)


def _oproj_body(mg_ref, x_ref, w_ref, g2_ref, mod_ref, x1_ref, h2_ref):
    gate = mod_ref[0, 2:3, :]
    shift = mod_ref[0, 3:4, :]
    scale = mod_ref[0, 4:5, :]
    for r0 in range(0, x_ref.shape[0], CHUNK):
        rows = slice(r0, r0 + CHUNK)
        x1 = x_ref[rows, :] + gate * jnp.dot(mg_ref[rows, :], w_ref[...],
                                             preferred_element_type=F32)
        x1_ref[rows, :] = x1
        h2_ref[rows, :] = (_rms_rows(x1) * g2_ref[...] * (1.0 + scale) + shift).astype(BF16)


def _oproj(merged, x2, w_o, g2, mod3, seq, bm=512):
    m, d = x2.shape
    per = seq // bm
    row = pl.BlockSpec((bm, d), lambda i: (i, 0))
    return pl.pallas_call(
        _oproj_body,
        grid=(m // bm,),
        in_specs=[row, row,
                  pl.BlockSpec((d, d), lambda i: (0, 0), pipeline_mode=pl.Buffered(1)),
                  pl.BlockSpec((1, d), lambda i: (0, 0)),
                  pl.BlockSpec((1, N_MOD, d), lambda i: (i // per, 0, 0))],
        out_specs=[row, row],
        out_shape=[jax.ShapeDtypeStruct((m, d), F32), jax.ShapeDtypeStruct((m, d), BF16)],
        compiler_params=_params("parallel"),
        name="oproj",
    )(merged, x2, w_o, g2, mod3)


def _down_body(g_ref, x1_ref, w_ref, gf_ref, mod_ref, o_ref):
    gate = mod_ref[0, 5:6, :]
    x2 = x1_ref[...] + gate * jnp.dot(g_ref[...], w_ref[...], preferred_element_type=F32)
    o_ref[...] = _rms_rows(x2) * gf_ref[...]


def _down(gff, x1, w_down, g_final, mod3, seq, bm=256):
    m, d = x1.shape
    dff = gff.shape[1]
    per = seq // bm
    row = pl.BlockSpec((bm, d), lambda i: (i, 0))
    return pl.pallas_call(
        _down_body,
        grid=(m // bm,),
        in_specs=[pl.BlockSpec((bm, dff), lambda i: (i, 0)), row,
                  pl.BlockSpec((dff, d), lambda i: (0, 0), pipeline_mode=pl.Buffered(1)),
                  pl.BlockSpec((1, d), lambda i: (0, 0)),
                  pl.BlockSpec((1, N_MOD, d), lambda i: (i // per, 0, 0))],
        out_specs=row,
        out_shape=jax.ShapeDtypeStruct((m, d), F32),
        compiler_params=_params("parallel"),
        name="down",
    )(gff, x1, w_down, g_final, mod3)


def _layer(x, c, w_ada, b_ada, g_norm1, w_in, w_ssm_conv, b_ssm_conv, dt_bias, a_log, d_skip,
           g_ssm_norm, w_ssm_out, w_cfm_dw, b_cfm_dw, g_cfm_ln, b_cfm_ln, w_cfm_out, b_cfm_out,
           w_o, g_norm2, w_up, w_ff_conv, b_ff_conv, w_down, g_final):
    bsz, seq, d = x.shape
    d_inner = w_ssm_out.shape[0]
    heads = dt_bias.shape[0]
    xbc_dim = w_ssm_conv.shape[1]
    cfm_d = w_cfm_out.shape[0]
    d_ff = w_down.shape[0]
    row = lambda v: v.reshape(1, -1).astype(F32)

    o_xbc = d_inner
    o_dt = o_xbc + xbc_dim
    o_lin = o_dt + heads
    o_gate = o_lin + cfm_d
    o_g = o_gate + cfm_d
    w_in_b = w_in.astype(BF16)
    w_tail = w_in_b[:, o_lin:]
    w_up_b = w_up.astype(BF16)
    w_dt = jnp.pad(w_in[:, o_dt:o_lin], ((0, 0), (0, LANES - heads))).astype(BF16)
    dt_bias_pad = jnp.pad(row(dt_bias), ((0, 0), (0, LANES - heads)))
    a_log_pad = jnp.pad(row(a_log), ((0, 0), (0, LANES - heads)))
    dskip_cols = jnp.repeat(d_skip.astype(F32), SSM_HEAD_DIM).reshape(1, d_inner)

    x2 = x.reshape(bsz * seq, d)
    mod3 = _ada(c, w_ada, row(b_ada)).reshape(bsz, N_MOD, d)
    h = _norm1(x2, row(g_norm1), mod3, seq)

    zs = _seqmm_ew(h, Cols(w_in_b, 0, d_inner), None, _silu, BF16, seq, 512, "in_z")
    xbc = _seqmm_conv(h, [Cols(w_in_b, o_xbc, xbc_dim)], w_ssm_conv.astype(F32),
                      row(b_ssm_conv), "silu", seq, 512, "in_xbc")
    dt = _seqmm_ew(h, Cols(w_dt, 0, LANES), dt_bias_pad, _softplus, F32, seq, LANES, "in_dt")
    uc = _seqmm_conv(h, [Cols(w_tail, 0, cfm_d), Cols(w_tail, cfm_d, cfm_d)],
                     w_cfm_dw.astype(F32), row(b_cfm_dw), "glu", seq, 512, "in_cfm")
    gates = _seqmm_ew(h, Cols(w_tail, 2 * cfm_d, w_in.shape[1] - o_g), None, _sigmoid, BF16,
                      seq, 512, "in_gates")

    yn = _ssd(xbc, dt, zs, a_log_pad, dskip_cols, row(g_ssm_norm), seq)
    merged = _merge(yn, uc, gates, w_ssm_out.astype(BF16), w_cfm_out.astype(BF16),
                    row(g_cfm_ln), row(b_cfm_ln), row(b_cfm_out))
    x1, h2 = _oproj(merged, x2, w_o.astype(BF16), row(g_norm2), mod3, seq)

    gff = _seqmm_conv(h2, [Cols(w_up_b, 0, d_ff), Cols(w_up_b, d_ff, d_ff)],
                      w_ff_conv.astype(F32), row(b_ff_conv), "gate", seq, 512, "ffn_up")
    out = _down(gff, x1, w_down.astype(BF16), row(g_final), mod3, seq)
    return out.reshape(bsz, seq, d)


def kernel(x, c, w_ada, b_ada, g_norm1, w_in, w_ssm_conv, b_ssm_conv, dt_bias, a_log, d_skip, g_ssm_norm, w_ssm_out, w_cfm_dw, b_cfm_dw, g_cfm_ln, b_cfm_ln, w_cfm_out, b_cfm_out, w_o, g_norm2, w_up, w_ff_conv, b_ff_conv, w_down, g_final):
    assert w_ada.shape[0] == 1, "single-layer stack"
    return _layer(x, c, w_ada[0], b_ada[0], g_norm1[0], w_in[0], w_ssm_conv[0], b_ssm_conv[0],
                  dt_bias[0], a_log[0], d_skip[0], g_ssm_norm[0], w_ssm_out[0], w_cfm_dw[0],
                  b_cfm_dw[0], g_cfm_ln[0], b_cfm_ln[0], w_cfm_out[0], b_cfm_out[0], w_o[0],
                  g_norm2[0], w_up[0], w_ff_conv[0], b_ff_conv[0], w_down[0], g_final)
```

```python
import functools
from typing import NamedTuple

import jax
import jax.numpy as jnp
from jax import lax
from jax.experimental import pallas as pl
from jax.experimental.pallas import tpu as pltpu

F32 = jnp.float32
BF16 = jnp.bfloat16

SSM_HEAD_DIM = 64
SSM_GROUPS = 8
SSM_STATE = 128
SSM_CHUNK = 256
N_MOD = 6
RMS_EPS = 1e-6
LN_EPS = 1e-5

LANES = 128
SUBLANES = 8
CHUNK = 256
DOT_ROWS = 256
HALO = 256
SUB_ROWS = 64
PARTIALS = 4
VMEM_LIMIT = 56 * 1024 * 1024


def _params(*sem):
    return pltpu.CompilerParams(dimension_semantics=sem, vmem_limit_bytes=VMEM_LIMIT)


def _sigmoid(v):
    return 1.0 / (1.0 + jnp.exp(-v))


def _silu(v):
    return v * _sigmoid(v)


def _softplus(v):
    return jnp.maximum(v, 0.0) + jnp.log1p(jnp.exp(-jnp.abs(v)))


def _rms_rows(v):
    return v * lax.rsqrt(jnp.mean(v * v, axis=-1, keepdims=True) + RMS_EPS)


def _ada_body(c_ref, w_ref, b_ref, o_ref):
    s = _silu(c_ref[...]).astype(BF16)
    o_ref[...] = jnp.dot(s, w_ref[...].astype(BF16), preferred_element_type=F32) + b_ref[...]


def _ada(c, w, b, bn=1024):
    bsz, d = c.shape
    n = w.shape[1]
    return pl.pallas_call(
        _ada_body,
        grid=(n // bn,),
        in_specs=[pl.BlockSpec((bsz, d), lambda j: (0, 0)),
                  pl.BlockSpec((d, bn), lambda j: (0, j)),
                  pl.BlockSpec((1, bn), lambda j: (0, j))],
        out_specs=pl.BlockSpec((bsz, bn), lambda j: (0, j)),
        out_shape=jax.ShapeDtypeStruct((bsz, n), F32),
        compiler_params=_params("arbitrary"),
        name="ada",
    )(c, w, b)


def _norm1_body(x_ref, g_ref, mod_ref, o_ref):
    shift = mod_ref[0, 0:1, :]
    scale = mod_ref[0, 1:2, :]
    y = _rms_rows(x_ref[...]) * g_ref[...]
    o_ref[...] = (y * (1.0 + scale) + shift).astype(BF16)


def _norm1(x2, g, mod3, seq, bm=512):
    m, d = x2.shape
    per = seq // bm
    return pl.pallas_call(
        _norm1_body,
        grid=(m // bm,),
        in_specs=[pl.BlockSpec((bm, d), lambda i: (i, 0)),
                  pl.BlockSpec((1, d), lambda i: (0, 0)),
                  pl.BlockSpec((1, N_MOD, d), lambda i: (i // per, 0, 0))],
        out_specs=pl.BlockSpec((bm, d), lambda i: (i, 0)),
        out_shape=jax.ShapeDtypeStruct((m, d), BF16),
        compiler_params=_params("parallel"),
        name="norm1",
    )(x2, g, mod3)


class Cols(NamedTuple):
    arr: jax.Array
    start: int
    width: int


def _seq_specs(h, ws, consts, seq, bn):
    k = h.shape[1]
    in_specs = [pl.BlockSpec((seq, k), lambda b, j: (b, 0))]
    for w in ws:
        first, rem = divmod(w.start, bn)
        assert rem == 0 and w.width % bn == 0
        in_specs.append(pl.BlockSpec((k, bn), lambda b, j, first=first: (0, first + j)))
    in_specs += [pl.BlockSpec((cst.shape[0], bn), lambda b, j: (0, j)) for cst in consts]
    return in_specs


def _seqmm_ew_body(seq, fn, has_bias, h_ref, w_ref, *refs):
    o_ref = refs[-1]
    for i in range(seq // CHUNK):
        rows = slice(i * CHUNK, (i + 1) * CHUNK)
        a = jnp.dot(h_ref[rows, :], w_ref[...], preferred_element_type=F32)
        if has_bias:
            a = a + refs[0][...]
        o_ref[rows, :] = fn(a).astype(o_ref.dtype)


def _seqmm_ew(h, w, bias, fn, out_dtype, seq, bn, name):
    m = h.shape[0]
    n = w.width
    consts = [] if bias is None else [bias]
    return pl.pallas_call(
        functools.partial(_seqmm_ew_body, seq, fn, bias is not None),
        grid=(m // seq, n // bn),
        in_specs=_seq_specs(h, [w], consts, seq, bn),
        out_specs=pl.BlockSpec((seq, bn), lambda b, j: (b, j)),
        out_shape=jax.ShapeDtypeStruct((m, n), out_dtype),
        compiler_params=_params("parallel", "arbitrary"),
        name=name,
    )(h, w.arr, *consts)


def _seqmm_conv_body(seq, width, kind, drows, h_ref, w0_ref, *refs):
    two = kind in ("glu", "gate")
    w1_ref = refs[0] if two else None
    cw_ref, cb_ref, o_ref, u_ref, y_ref = refs[two:two + 5]
    v_ref = refs[two + 5] if kind == "gate" else None
    tlen = seq // SUBLANES
    dseg = drows // SUBLANES
    ndot = seq // drows
    ncol = o_ref.shape[1] // LANES

    def dot_chunk(i):
        lhs = jnp.concatenate(
            [h_ref[s * tlen + i * dseg:s * tlen + (i + 1) * dseg, :] for s in range(SUBLANES)],
            axis=0)
        a = jnp.dot(lhs, w0_ref[...], preferred_element_type=F32)
        if kind == "glu":
            a = a * _sigmoid(jnp.dot(lhs, w1_ref[...], preferred_element_type=F32))
        for c in range(ncol):
            for s in range(SUBLANES):
                u_ref[c, pl.ds(HALO + i * drows + s, dseg, stride=SUBLANES), :] = (
                    a[s * dseg:(s + 1) * dseg, c * LANES:(c + 1) * LANES])
        if kind == "gate":
            v_ref[i * drows:(i + 1) * drows, :] = jnp.dot(lhs, w1_ref[...],
                                                         preferred_element_type=F32)

    def conv_block(c, row0, rows):
        parts = [None] * min(PARTIALS, width)
        for back in range(width):
            tap = width - 1 - back
            xs = u_ref[c, pl.ds(HALO + row0 - SUBLANES * back, rows), :]
            term = cw_ref[c, tap:tap + 1, :] * xs
            k = back % len(parts)
            parts[k] = term if parts[k] is None else parts[k] + term
        while len(parts) > 1:
            parts = [a + b for a, b in zip(parts[0::2], parts[1::2])] + parts[len(parts) & ~1:]
        y_ref[c, pl.ds(row0, rows), :] = parts[0] + cb_ref[c]

    def conv_chunk(i):
        r = i * drows
        for c in range(ncol):
            for rs in range(0, drows, SUB_ROWS):
                conv_block(c, r + rs, SUB_ROWS)
        for c in range(ncol):
            cols = slice(c * LANES, (c + 1) * LANES)
            for s in range(SUBLANES):
                y = y_ref[c, pl.ds(r + s, dseg, stride=SUBLANES), :]
                if kind == "silu":
                    y = _silu(y)
                elif kind == "gate":
                    y = _silu(y) * v_ref[r + s * dseg:r + (s + 1) * dseg, cols]
                o_ref[s * tlen + i * dseg:s * tlen + (i + 1) * dseg, cols] = y.astype(o_ref.dtype)

    dot_chunk(ndot - 1)
    first = lax.broadcasted_iota(jnp.int32, (HALO, LANES), 0) % SUBLANES == 0
    for c in range(ncol):
        tail = pltpu.roll(u_ref[c, seq:HALO + seq, :], 1, axis=0)
        u_ref[c, 0:HALO, :] = jnp.where(first, 0.0, tail)
    dot_chunk(0)
    for i in range(ndot - 2):
        dot_chunk(i + 1)
        conv_chunk(i)
    conv_chunk(ndot - 2)
    conv_chunk(ndot - 1)


def _seqmm_conv(h, ws, conv_w, conv_b, kind, seq, bn, name):
    m = h.shape[0]
    n = ws[0].width
    assert seq % DOT_ROWS == 0 and seq // SUBLANES >= HALO // SUBLANES >= conv_w.shape[0] - 1
    scratch = [pltpu.VMEM((bn // LANES, HALO + seq + SUBLANES, LANES), F32),
               pltpu.VMEM((bn // LANES, seq + SUBLANES, LANES), F32)]
    if kind == "gate":
        scratch.append(pltpu.VMEM((seq, bn), F32))
    width = conv_w.shape[0]
    cw = conv_w.reshape(width, n // LANES, LANES).transpose(1, 0, 2)
    cb = conv_b.reshape(n // LANES, 1, LANES)
    col_spec = lambda rows: pl.BlockSpec((bn // LANES, rows, LANES), lambda b, j: (j, 0, 0))
    return pl.pallas_call(
        functools.partial(_seqmm_conv_body, seq, width, kind, DOT_ROWS),
        grid=(m // seq, n // bn),
        in_specs=_seq_specs(h, ws, [], seq, bn) + [col_spec(width), col_spec(1)],
        out_specs=pl.BlockSpec((seq, bn), lambda b, j: (b, j)),
        out_shape=jax.ShapeDtypeStruct((m, n), BF16),
        scratch_shapes=scratch,
        compiler_params=_params("parallel", "arbitrary"),
        name=name,
    )(h, *[w.arr for w in ws], cw, cb)


def _ssd_body(x_ref, b_ref, c_ref, dt_ref, z_ref, alog_ref, dskip_ref, gn_ref, e_ref, o_ref,
              state_ref, ex_ref):
    q = SSM_CHUNK
    p = SSM_HEAD_DIM
    n = SSM_STATE
    heads_per_group = x_ref.shape[1] // (SSM_GROUPS * p)
    gw = heads_per_group * p

    @pl.when(pl.program_id(1) == 0)
    def _():
        state_ref[...] = jnp.zeros_like(state_ref)

    dt = dt_ref[...]
    dta = dt * (-jnp.exp(alog_ref[...]))
    row = lax.broadcasted_iota(jnp.int32, (q, q), 0)
    col = lax.broadcasted_iota(jnp.int32, (q, q), 1)
    causal = row >= col
    a_cum = jnp.dot(causal.astype(F32), dta, precision=lax.Precision.HIGHEST,
                    preferred_element_type=F32)
    a_cum_t = a_cum.T
    a_last = a_cum[q - 1:q, :]

    per_head = jnp.concatenate(
        [dt, jnp.exp(a_cum), dt * jnp.exp(a_last - a_cum),
         jnp.broadcast_to(jnp.exp(a_last), (SUBLANES, LANES))], axis=0)
    hi = per_head.astype(BF16)
    lo = (per_head - hi.astype(F32)).astype(BF16)
    ex_ref[...] = (jnp.dot(hi, e_ref[...], preferred_element_type=F32)
                   + jnp.dot(lo, e_ref[...], preferred_element_type=F32))

    lane_head = lax.broadcasted_iota(jnp.int32, (1, gw), 1) // p
    for g in range(SSM_GROUPS):
        cols = slice(g * gw, (g + 1) * gw)
        bg = b_ref[:, g * n:(g + 1) * n]
        cg = c_ref[:, g * n:(g + 1) * n]
        cb = lax.dot_general(cg, bg, (((1,), (1,)), ((), ())), preferred_element_type=F32)
        prev = state_ref[g]
        xg = x_ref[:, cols].astype(F32)
        xdt = xg * ex_ref[0:q, cols]
        lhs, rhs = [], []
        for r in range(heads_per_group):
            hd = g * heads_per_group + r
            seg = a_cum[:, hd:hd + 1] - a_cum_t[hd:hd + 1, :]
            decay = jnp.exp(jnp.where(causal, seg, -jnp.inf))
            lhs.append((decay * cb).astype(BF16))
            rhs.append(jnp.where(lane_head == r, xdt, 0.0).astype(BF16))
        y = jnp.dot(jnp.concatenate(lhs, axis=1), jnp.concatenate(rhs, axis=0),
                    preferred_element_type=F32)
        y = y + jnp.dot(cg, prev.astype(BF16), preferred_element_type=F32) * ex_ref[q:2 * q, cols]
        y = y + xg * dskip_ref[:, cols]
        xend = (xg * ex_ref[2 * q:3 * q, cols]).astype(BF16)
        new = lax.dot_general(bg, xend, (((0,), (0,)), ((), ())), preferred_element_type=F32)
        state_ref[g] = prev * ex_ref[3 * q:3 * q + 1, cols] + new
        yz = y * z_ref[:, cols].astype(F32)
        o_ref[:, cols] = (_rms_rows(yz) * gn_ref[:, cols]).astype(o_ref.dtype)


def _ssd(xbc, dt, zs, a_log_pad, dskip_cols, g_norm, seq):
    m = xbc.shape[0]
    d_inner = zs.shape[1]
    bc_w = SSM_GROUPS * SSM_STATE
    nchunk = seq // SSM_CHUNK
    q = SSM_CHUNK
    rows = lambda b, c: b * nchunk + c
    spread = (jnp.arange(d_inner)[None, :] // SSM_HEAD_DIM == jnp.arange(LANES)[:, None]).astype(BF16)
    return pl.pallas_call(
        _ssd_body,
        grid=(m // seq, nchunk),
        in_specs=[pl.BlockSpec((q, d_inner), lambda b, c: (rows(b, c), 0)),
                  pl.BlockSpec((q, bc_w), lambda b, c: (rows(b, c), d_inner // bc_w)),
                  pl.BlockSpec((q, bc_w), lambda b, c: (rows(b, c), d_inner // bc_w + 1)),
                  pl.BlockSpec((q, LANES), lambda b, c: (rows(b, c), 0)),
                  pl.BlockSpec((q, d_inner), lambda b, c: (rows(b, c), 0)),
                  pl.BlockSpec((1, LANES), lambda b, c: (0, 0)),
                  pl.BlockSpec((1, d_inner), lambda b, c: (0, 0)),
                  pl.BlockSpec((1, d_inner), lambda b, c: (0, 0)),
                  pl.BlockSpec((LANES, d_inner), lambda b, c: (0, 0))],
        out_specs=pl.BlockSpec((q, d_inner), lambda b, c: (rows(b, c), 0)),
        out_shape=jax.ShapeDtypeStruct((m, d_inner), BF16),
        scratch_shapes=[pltpu.VMEM((SSM_GROUPS, SSM_STATE, d_inner // SSM_GROUPS), F32),
                        pltpu.VMEM((3 * q + SUBLANES, d_inner), F32)],
        compiler_params=_params("parallel", "arbitrary"),
        name="ssd",
    )(xbc, xbc, xbc, dt, zs, a_log_pad, dskip_cols, g_norm, spread)


def _merge_body(yn_ref, uc_ref, ga_ref, gb_ref, wa_ref, wb_ref, lng_ref, lnb_ref, bo_ref, o_ref):
    for r0 in range(0, o_ref.shape[0], CHUNK):
        rows = slice(r0, r0 + CHUNK)
        u = uc_ref[rows, :].astype(F32)
        mu = jnp.mean(u, axis=-1, keepdims=True)
        ctr = u - mu
        var = jnp.mean(ctr * ctr, axis=-1, keepdims=True)
        un = _silu(ctr * lax.rsqrt(var + LN_EPS) * lng_ref[...] + lnb_ref[...]).astype(BF16)
        ya = jnp.dot(yn_ref[rows, :], wa_ref[...], preferred_element_type=F32)
        yb = jnp.dot(un, wb_ref[...], preferred_element_type=F32) + bo_ref[...]
        o_ref[rows, :] = (ga_ref[rows, :].astype(F32) * ya
                          + gb_ref[rows, :].astype(F32) * yb).astype(BF16)


def _merge(yn, uc, gates, wa, wb, ln_g, ln_b, b_out, bm=512):
    m, d = yn.shape
    row = pl.BlockSpec((bm, d), lambda i: (i, 0))
    vec = pl.BlockSpec((1, d), lambda i: (0, 0))
    wspec = pl.BlockSpec((d, d), lambda i: (0, 0), pipeline_mode=pl.Buffered(1))
    return pl.pallas_call(
        _merge_body,
        grid=(m // bm,),
        in_specs=[row, row, row, pl.BlockSpec((bm, d), lambda i: (i, 1)), wspec, wspec,
                  vec, vec, vec],
        out_specs=row,
        out_shape=jax.ShapeDtypeStruct((m, d), BF16),
        compiler_params=_params("parallel"),
        name="merge",
    )(yn, uc, gates, gates, wa, wb, ln_g, ln_b, b_out)


def _oproj_body(mg_ref, x_ref, w_ref, g2_ref, mod_ref, x1_ref, h2_ref):
    gate = mod_ref[0, 2:3, :]
    shift = mod_ref[0, 3:4, :]
    scale = mod_ref[0, 4:5, :]
    for r0 in range(0, x_ref.shape[0], CHUNK):
        rows = slice(r0, r0 + CHUNK)
        x1 = x_ref[rows, :] + gate * jnp.dot(mg_ref[rows, :], w_ref[...],
                                             preferred_element_type=F32)
        x1_ref[rows, :] = x1
        h2_ref[rows, :] = (_rms_rows(x1) * g2_ref[...] * (1.0 + scale) + shift).astype(BF16)


def _oproj(merged, x2, w_o, g2, mod3, seq, bm=512):
    m, d = x2.shape
    per = seq // bm
    row = pl.BlockSpec((bm, d), lambda i: (i, 0))
    return pl.pallas_call(
        _oproj_body,
        grid=(m // bm,),
        in_specs=[row, row,
                  pl.BlockSpec((d, d), lambda i: (0, 0), pipeline_mode=pl.Buffered(1)),
                  pl.BlockSpec((1, d), lambda i: (0, 0)),
                  pl.BlockSpec((1, N_MOD, d), lambda i: (i // per, 0, 0))],
        out_specs=[row, row],
        out_shape=[jax.ShapeDtypeStruct((m, d), F32), jax.ShapeDtypeStruct((m, d), BF16)],
        compiler_params=_params("parallel"),
        name="oproj",
    )(merged, x2, w_o, g2, mod3)


def _down_body(g_ref, x1_ref, w_ref, gf_ref, mod_ref, o_ref):
    gate = mod_ref[0, 5:6, :]
    x2 = x1_ref[...] + gate * jnp.dot(g_ref[...], w_ref[...], preferred_element_type=F32)
    o_ref[...] = _rms_rows(x2) * gf_ref[...]


def _down(gff, x1, w_down, g_final, mod3, seq, bm=256):
    m, d = x1.shape
    dff = gff.shape[1]
    per = seq // bm
    row = pl.BlockSpec((bm, d), lambda i: (i, 0))
    return pl.pallas_call(
        _down_body,
        grid=(m // bm,),
        in_specs=[pl.BlockSpec((bm, dff), lambda i: (i, 0)), row,
                  pl.BlockSpec((dff, d), lambda i: (0, 0), pipeline_mode=pl.Buffered(1)),
                  pl.BlockSpec((1, d), lambda i: (0, 0)),
                  pl.BlockSpec((1, N_MOD, d), lambda i: (i // per, 0, 0))],
        out_specs=row,
        out_shape=jax.ShapeDtypeStruct((m, d), F32),
        compiler_params=_params("parallel"),
        name="down",
    )(gff, x1, w_down, g_final, mod3)


def _layer(x, c, w_ada, b_ada, g_norm1, w_in, w_ssm_conv, b_ssm_conv, dt_bias, a_log, d_skip,
           g_ssm_norm, w_ssm_out, w_cfm_dw, b_cfm_dw, g_cfm_ln, b_cfm_ln, w_cfm_out, b_cfm_out,
           w_o, g_norm2, w_up, w_ff_conv, b_ff_conv, w_down, g_final):
    bsz, seq, d = x.shape
    d_inner = w_ssm_out.shape[0]
    heads = dt_bias.shape[0]
    xbc_dim = w_ssm_conv.shape[1]
    cfm_d = w_cfm_out.shape[0]
    d_ff = w_down.shape[0]
    row = lambda v: v.reshape(1, -1).astype(F32)

    o_xbc = d_inner
    o_dt = o_xbc + xbc_dim
    o_lin = o_dt + heads
    o_gate = o_lin + cfm_d
    o_g = o_gate + cfm_d
    w_in_b = w_in.astype(BF16)
    w_tail = w_in_b[:, o_lin:]
    w_up_b = w_up.astype(BF16)
    w_dt = jnp.pad(w_in[:, o_dt:o_lin], ((0, 0), (0, LANES - heads))).astype(BF16)
    dt_bias_pad = jnp.pad(row(dt_bias), ((0, 0), (0, LANES - heads)))
    a_log_pad = jnp.pad(row(a_log), ((0, 0), (0, LANES - heads)))
    dskip_cols = jnp.repeat(d_skip.astype(F32), SSM_HEAD_DIM).reshape(1, d_inner)

    x2 = x.reshape(bsz * seq, d)
    mod3 = _ada(c, w_ada, row(b_ada)).reshape(bsz, N_MOD, d)
    h = _norm1(x2, row(g_norm1), mod3, seq)

    zs = _seqmm_ew(h, Cols(w_in_b, 0, d_inner), None, _silu, BF16, seq, 512, "in_z")
    xbc = _seqmm_conv(h, [Cols(w_in_b, o_xbc, xbc_dim)], w_ssm_conv.astype(F32),
                      row(b_ssm_conv), "silu", seq, 512, "in_xbc")
    dt = _seqmm_ew(h, Cols(w_dt, 0, LANES), dt_bias_pad, _softplus, F32, seq, LANES, "in_dt")
    uc = _seqmm_conv(h, [Cols(w_tail, 0, cfm_d), Cols(w_tail, cfm_d, cfm_d)],
                     w_cfm_dw.astype(F32), row(b_cfm_dw), "glu", seq, 512, "in_cfm")
    gates = _seqmm_ew(h, Cols(w_tail, 2 * cfm_d, w_in.shape[1] - o_g), None, _sigmoid, BF16,
                      seq, 512, "in_gates")

    yn = _ssd(xbc, dt, zs, a_log_pad, dskip_cols, row(g_ssm_norm), seq)
    merged = _merge(yn, uc, gates, w_ssm_out.astype(BF16), w_cfm_out.astype(BF16),
                    row(g_cfm_ln), row(b_cfm_ln), row(b_cfm_out))
    x1, h2 = _oproj(merged, x2, w_o.astype(BF16), row(g_norm2), mod3, seq)

    gff = _seqmm_conv(h2, [Cols(w_up_b, 0, d_ff), Cols(w_up_b, d_ff, d_ff)],
                      w_ff_conv.astype(F32), row(b_ff_conv), "gate", seq, 512, "ffn_up")
    out = _down(gff, x1, w_down.astype(BF16), row(g_final), mod3, seq)
    return out.reshape(bsz, seq, d)


def kernel(x, c, w_ada, b_ada, g_norm1, w_in, w_ssm_conv, b_ssm_conv, dt_bias, a_log, d_skip, g_ssm_norm, w_ssm_out, w_cfm_dw, b_cfm_dw, g_cfm_ln, b_cfm_ln, w_cfm_out, b_cfm_out, w_o, g_norm2, w_up, w_ff_conv, b_ff_conv, w_down, g_final):
    assert w_ada.shape[0] == 1, "single-layer stack"
    return _layer(x, c, w_ada[0], b_ada[0], g_norm1[0], w_in[0], w_ssm_conv[0], b_ssm_conv[0],
                  dt_bias[0], a_log[0], d_skip[0], g_ssm_norm[0], w_ssm_out[0], w_cfm_dw[0],
                  b_cfm_dw[0], g_cfm_ln[0], b_cfm_ln[0], w_cfm_out[0], b_cfm_out[0], w_o[0],
                  g_norm2[0], w_up[0], w_ff_conv[0], b_ff_conv[0], w_down[0], g_final)
```

```python
import functools
from typing import NamedTuple

import jax
import jax.numpy as jnp
from jax import lax
from jax.experimental import pallas as pl
from jax.experimental.pallas import tpu as pltpu

F32 = jnp.float32
BF16 = jnp.bfloat16

SSM_HEAD_DIM = 64
SSM_GROUPS = 8
SSM_STATE = 128
SSM_CHUNK = 256
N_MOD = 6
RMS_EPS = 1e-6
LN_EPS = 1e-5

LANES = 128
SUBLANES = 8
CHUNK = 256
COL_BLOCK = 512
HALO = 256
SUB_ROWS = 64
PARTIALS = 4
VMEM_LIMIT = 56 * 1024 * 1024


def _params(*sem):
    return pltpu.CompilerParams(dimension_semantics=sem, vmem_limit_bytes=VMEM_LIMIT)


def _sigmoid(v):
    return 1.0 / (1.0 + jnp.exp(-v))


def _silu(v):
    return v * _sigmoid(v)


def _softplus(v):
    return jnp.maximum(v, 0.0) + jnp.log1p(jnp.exp(-jnp.abs(v)))


def _rms_rows(v):
    return v * lax.rsqrt(jnp.mean(v * v, axis=-1, keepdims=True) + RMS_EPS)


def _ada_body(c_ref, w_ref, b_ref, o_ref):
    s = _silu(c_ref[...]).astype(BF16)
    o_ref[...] = jnp.dot(s, w_ref[...].astype(BF16), preferred_element_type=F32) + b_ref[...]


def _ada(c, w, b, bn=1024):
    bsz, d = c.shape
    n = w.shape[1]
    return pl.pallas_call(
        _ada_body,
        grid=(n // bn,),
        in_specs=[pl.BlockSpec((bsz, d), lambda j: (0, 0)),
                  pl.BlockSpec((d, bn), lambda j: (0, j)),
                  pl.BlockSpec((1, bn), lambda j: (0, j))],
        out_specs=pl.BlockSpec((bsz, bn), lambda j: (0, j)),
        out_shape=jax.ShapeDtypeStruct((bsz, n), F32),
        compiler_params=_params("arbitrary"),
        name="ada",
    )(c, w, b)


def _norm1_body(x_ref, g_ref, mod_ref, o_ref):
    shift = mod_ref[0, 0:1, :]
    scale = mod_ref[0, 1:2, :]
    y = _rms_rows(x_ref[...]) * g_ref[...]
    o_ref[...] = (y * (1.0 + scale) + shift).astype(BF16)


def _norm1(x2, g, mod3, seq, bm=512):
    m, d = x2.shape
    per = seq // bm
    return pl.pallas_call(
        _norm1_body,
        grid=(m // bm,),
        in_specs=[pl.BlockSpec((bm, d), lambda i: (i, 0)),
                  pl.BlockSpec((1, d), lambda i: (0, 0)),
                  pl.BlockSpec((1, N_MOD, d), lambda i: (i // per, 0, 0))],
        out_specs=pl.BlockSpec((bm, d), lambda i: (i, 0)),
        out_shape=jax.ShapeDtypeStruct((m, d), BF16),
        compiler_params=_params("parallel"),
        name="norm1",
    )(x2, g, mod3)


class Cols(NamedTuple):
    arr: jax.Array
    start: int
    width: int


def _seq_specs(h, ws, consts, seq, bn):
    k = h.shape[1]
    in_specs = [pl.BlockSpec((seq, k), lambda b, j: (b, 0))]
    for w in ws:
        first, rem = divmod(w.start, bn)
        assert rem == 0 and w.width % bn == 0
        in_specs.append(pl.BlockSpec((k, bn), lambda b, j, first=first: (0, first + j)))
    in_specs += [pl.BlockSpec((cst.shape[0], bn), lambda b, j: (0, j)) for cst in consts]
    return in_specs


def _seqmm_ew_body(seq, fn, has_bias, h_ref, w_ref, *refs):
    o_ref = refs[-1]
    for i in range(seq // CHUNK):
        rows = slice(i * CHUNK, (i + 1) * CHUNK)
        a = jnp.dot(h_ref[rows, :], w_ref[...], preferred_element_type=F32)
        if has_bias:
            a = a + refs[0][...]
        o_ref[rows, :] = fn(a).astype(o_ref.dtype)


def _seqmm_ew(h, w, bias, fn, out_dtype, seq, bn, name):
    m = h.shape[0]
    n = w.width
    consts = [] if bias is None else [bias]
    return pl.pallas_call(
        functools.partial(_seqmm_ew_body, seq, fn, bias is not None),
        grid=(m // seq, n // bn),
        in_specs=_seq_specs(h, [w], consts, seq, bn),
        out_specs=pl.BlockSpec((seq, bn), lambda b, j: (b, j)),
        out_shape=jax.ShapeDtypeStruct((m, n), out_dtype),
        compiler_params=_params("parallel", "arbitrary"),
        name=name,
    )(h, w.arr, *consts)


def _seqmm_conv_body(seq, width, kind, drows, h_ref, w0_ref, *refs):
    two = kind in ("glu", "gate")
    w1_ref = refs[0] if two else None
    cw_ref, cb_ref, o_ref, u_ref, y_ref = refs[two:two + 5]
    v_ref = refs[two + 5] if kind == "gate" else None
    tlen = seq // SUBLANES
    dseg = drows // SUBLANES
    ndot = seq // drows
    ncol = o_ref.shape[1] // LANES

    def dot_chunk(i):
        lhs = jnp.concatenate(
            [h_ref[s * tlen + i * dseg:s * tlen + (i + 1) * dseg, :] for s in range(SUBLANES)],
            axis=0)
        a = jnp.dot(lhs, w0_ref[...], preferred_element_type=F32)
        if kind == "glu":
            a = a * _sigmoid(jnp.dot(lhs, w1_ref[...], preferred_element_type=F32))
        for c in range(ncol):
            for s in range(SUBLANES):
                u_ref[c, pl.ds(HALO + i * drows + s, dseg, stride=SUBLANES), :] = (
                    a[s * dseg:(s + 1) * dseg, c * LANES:(c + 1) * LANES])
        if kind == "gate":
            v_ref[i * drows:(i + 1) * drows, :] = jnp.dot(lhs, w1_ref[...],
                                                         preferred_element_type=F32)

    def conv_block(c, row0, rows):
        parts = [None] * min(PARTIALS, width)
        for back in range(width):
            tap = width - 1 - back
            xs = u_ref[c, pl.ds(HALO + row0 - SUBLANES * back, rows), :]
            term = cw_ref[c, tap:tap + 1, :] * xs
            k = back % len(parts)
            parts[k] = term if parts[k] is None else parts[k] + term
        while len(parts) > 1:
            parts = [a + b for a, b in zip(parts[0::2], parts[1::2])] + parts[len(parts) & ~1:]
        y_ref[c, pl.ds(row0, rows), :] = parts[0] + cb_ref[c]

    def conv_chunk(i):
        r = i * drows
        for c in range(ncol):
            for rs in range(0, drows, SUB_ROWS):
                conv_block(c, r + rs, SUB_ROWS)
        for c in range(ncol):
            cols = slice(c * LANES, (c + 1) * LANES)
            for s in range(SUBLANES):
                y = y_ref[c, pl.ds(r + s, dseg, stride=SUBLANES), :]
                if kind == "silu":
                    y = _silu(y)
                elif kind == "gate":
                    y = _silu(y) * v_ref[r + s * dseg:r + (s + 1) * dseg, cols]
                o_ref[s * tlen + i * dseg:s * tlen + (i + 1) * dseg, cols] = y.astype(o_ref.dtype)

    dot_chunk(ndot - 1)
    first = lax.broadcasted_iota(jnp.int32, (HALO, LANES), 0) % SUBLANES == 0
    for c in range(ncol):
        tail = pltpu.roll(u_ref[c, seq:HALO + seq, :], 1, axis=0)
        u_ref[c, 0:HALO, :] = jnp.where(first, 0.0, tail)
    dot_chunk(0)
    for i in range(ndot - 2):
        dot_chunk(i + 1)
        conv_chunk(i)
    conv_chunk(ndot - 2)
    conv_chunk(ndot - 1)


def _seqmm_conv(h, ws, conv_w, conv_b, kind, seq, bn, name):
    m = h.shape[0]
    n = ws[0].width
    assert seq % CHUNK == 0 and seq // SUBLANES >= HALO // SUBLANES >= conv_w.shape[0] - 1
    scratch = [pltpu.VMEM((bn // LANES, HALO + seq, LANES), F32),
               pltpu.VMEM((bn // LANES, seq, LANES), F32)]
    if kind == "gate":
        scratch.append(pltpu.VMEM((seq, bn), F32))
    width = conv_w.shape[0]
    cw = conv_w.reshape(width, n // LANES, LANES).transpose(1, 0, 2)
    cb = conv_b.reshape(n // LANES, 1, LANES)
    col_spec = lambda rows: pl.BlockSpec((bn // LANES, rows, LANES), lambda b, j: (j, 0, 0))
    return pl.pallas_call(
        functools.partial(_seqmm_conv_body, seq, width, kind, CHUNK),
        grid=(m // seq, n // bn),
        in_specs=_seq_specs(h, ws, [], seq, bn) + [col_spec(width), col_spec(1)],
        out_specs=pl.BlockSpec((seq, bn), lambda b, j: (b, j)),
        out_shape=jax.ShapeDtypeStruct((m, n), BF16),
        scratch_shapes=scratch,
        compiler_params=_params("parallel", "arbitrary"),
        name=name,
    )(h, *[w.arr for w in ws], cw, cb)


def _ssd_body(x_ref, b_ref, c_ref, dt_ref, z_ref, alog_ref, dskip_ref, gn_ref, e_ref, o_ref,
              state_ref, ex_ref):
    q = SSM_CHUNK
    p = SSM_HEAD_DIM
    n = SSM_STATE
    heads_per_group = x_ref.shape[1] // (SSM_GROUPS * p)
    gw = heads_per_group * p

    @pl.when(pl.program_id(1) == 0)
    def _():
        state_ref[...] = jnp.zeros_like(state_ref)

    dt = dt_ref[...]
    dta = dt * (-jnp.exp(alog_ref[...]))
    row = lax.broadcasted_iota(jnp.int32, (q, q), 0)
    col = lax.broadcasted_iota(jnp.int32, (q, q), 1)
    causal = row >= col
    a_cum = jnp.dot(causal.astype(F32), dta, precision=lax.Precision.HIGHEST,
                    preferred_element_type=F32)
    a_cum_t = a_cum.T
    a_last = a_cum[q - 1:q, :]

    per_head = jnp.concatenate(
        [dt, jnp.exp(a_cum), dt * jnp.exp(a_last - a_cum),
         jnp.broadcast_to(jnp.exp(a_last), (SUBLANES, LANES))], axis=0)
    hi = per_head.astype(BF16)
    lo = (per_head - hi.astype(F32)).astype(BF16)
    ex_ref[...] = (jnp.dot(hi, e_ref[...], preferred_element_type=F32)
                   + jnp.dot(lo, e_ref[...], preferred_element_type=F32))

    lane_head = lax.broadcasted_iota(jnp.int32, (1, gw), 1) // p
    for g in range(SSM_GROUPS):
        cols = slice(g * gw, (g + 1) * gw)
        bg = b_ref[:, g * n:(g + 1) * n]
        cg = c_ref[:, g * n:(g + 1) * n]
        cb = lax.dot_general(cg, bg, (((1,), (1,)), ((), ())), preferred_element_type=F32)
        prev = state_ref[g]
        xg = x_ref[:, cols].astype(F32)
        xdt = xg * ex_ref[0:q, cols]
        lhs, rhs = [], []
        for r in range(heads_per_group):
            hd = g * heads_per_group + r
            seg = a_cum[:, hd:hd + 1] - a_cum_t[hd:hd + 1, :]
            decay = jnp.exp(jnp.where(causal, seg, -jnp.inf))
            lhs.append((decay * cb).astype(BF16))
            rhs.append(jnp.where(lane_head == r, xdt, 0.0).astype(BF16))
        y = jnp.dot(jnp.concatenate(lhs, axis=1), jnp.concatenate(rhs, axis=0),
                    preferred_element_type=F32)
        y = y + jnp.dot(cg, prev.astype(BF16), preferred_element_type=F32) * ex_ref[q:2 * q, cols]
        y = y + xg * dskip_ref[:, cols]
        xend = (xg * ex_ref[2 * q:3 * q, cols]).astype(BF16)
        new = lax.dot_general(bg, xend, (((0,), (0,)), ((), ())), preferred_element_type=F32)
        state_ref[g] = prev * ex_ref[3 * q:3 * q + 1, cols] + new
        yz = y * z_ref[:, cols].astype(F32)
        o_ref[:, cols] = (_rms_rows(yz) * gn_ref[:, cols]).astype(o_ref.dtype)


def _ssd(xbc, dt, zs, a_log_pad, dskip_cols, g_norm, seq):
    m = xbc.shape[0]
    d_inner = zs.shape[1]
    bc_w = SSM_GROUPS * SSM_STATE
    nchunk = seq // SSM_CHUNK
    q = SSM_CHUNK
    rows = lambda b, c: b * nchunk + c
    spread = (jnp.arange(d_inner)[None, :] // SSM_HEAD_DIM == jnp.arange(LANES)[:, None]).astype(BF16)
    return pl.pallas_call(
        _ssd_body,
        grid=(m // seq, nchunk),
        in_specs=[pl.BlockSpec((q, d_inner), lambda b, c: (rows(b, c), 0)),
                  pl.BlockSpec((q, bc_w), lambda b, c: (rows(b, c), d_inner // bc_w)),
                  pl.BlockSpec((q, bc_w), lambda b, c: (rows(b, c), d_inner // bc_w + 1)),
                  pl.BlockSpec((q, LANES), lambda b, c: (rows(b, c), 0)),
                  pl.BlockSpec((q, d_inner), lambda b, c: (rows(b, c), 0)),
                  pl.BlockSpec((1, LANES), lambda b, c: (0, 0)),
                  pl.BlockSpec((1, d_inner), lambda b, c: (0, 0)),
                  pl.BlockSpec((1, d_inner), lambda b, c: (0, 0)),
                  pl.BlockSpec((LANES, d_inner), lambda b, c: (0, 0))],
        out_specs=pl.BlockSpec((q, d_inner), lambda b, c: (rows(b, c), 0)),
        out_shape=jax.ShapeDtypeStruct((m, d_inner), BF16),
        scratch_shapes=[pltpu.VMEM((SSM_GROUPS, SSM_STATE, d_inner // SSM_GROUPS), F32),
                        pltpu.VMEM((3 * q + SUBLANES, d_inner), F32)],
        compiler_params=_params("parallel", "arbitrary"),
        name="ssd",
    )(xbc, xbc, xbc, dt, zs, a_log_pad, dskip_cols, g_norm, spread)


def _merge_body(yn_ref, uc_ref, ga_ref, gb_ref, wa_ref, wb_ref, lng_ref, lnb_ref, bo_ref, o_ref):
    for r0 in range(0, o_ref.shape[0], CHUNK):
        rows = slice(r0, r0 + CHUNK)
        u = uc_ref[rows, :].astype(F32)
        mu = jnp.mean(u, axis=-1, keepdims=True)
        ctr = u - mu
        var = jnp.mean(ctr * ctr, axis=-1, keepdims=True)
        un = _silu(ctr * lax.rsqrt(var + LN_EPS) * lng_ref[...] + lnb_ref[...]).astype(BF16)
        ya = jnp.dot(yn_ref[rows, :], wa_ref[...], preferred_element_type=F32)
        yb = jnp.dot(un, wb_ref[...], preferred_element_type=F32) + bo_ref[...]
        o_ref[rows, :] = (ga_ref[rows, :].astype(F32) * ya
                          + gb_ref[rows, :].astype(F32) * yb).astype(BF16)


def _merge(yn, uc, gates, wa, wb, ln_g, ln_b, b_out, bm=512):
    m, d = yn.shape
    row = pl.BlockSpec((bm, d), lambda i: (i, 0))
    vec = pl.BlockSpec((1, d), lambda i: (0, 0))
    wspec = pl.BlockSpec((d, d), lambda i: (0, 0), pipeline_mode=pl.Buffered(1))
    return pl.pallas_call(
        _merge_body,
        grid=(m // bm,),
        in_specs=[row, row, row, pl.BlockSpec((bm, d), lambda i: (i, 1)), wspec, wspec,
                  vec, vec, vec],
        out_specs=row,
        out_shape=jax.ShapeDtypeStruct((m, d), BF16),
        compiler_params=_params("parallel"),
        name="merge",
    )(yn, uc, gates, gates, wa, wb, ln_g, ln_b, b_out)


def _oproj_body(mg_ref, x_ref, w_ref, g2_ref, mod_ref, x1_ref, h2_ref):
    gate = mod_ref[0, 2:3, :]
    shift = mod_ref[0, 3:4, :]
    scale = mod_ref[0, 4:5, :]
    for r0 in range(0, x_ref.shape[0], CHUNK):
        rows = slice(r0, r0 + CHUNK)
        x1 = x_ref[rows, :] + gate * jnp.dot(mg_ref[rows, :], w_ref[...],
                                             preferred_element_type=F32)
        x1_ref[rows, :] = x1
        h2_ref[rows, :] = (_rms_rows(x1) * g2_ref[...] * (1.0 + scale) + shift).astype(BF16)


def _oproj(merged, x2, w_o, g2, mod3, seq, bm=512):
    m, d = x2.shape
    per = seq // bm
    row = pl.BlockSpec((bm, d), lambda i: (i, 0))
    return pl.pallas_call(
        _oproj_body,
        grid=(m // bm,),
        in_specs=[row, row,
                  pl.BlockSpec((d, d), lambda i: (0, 0), pipeline_mode=pl.Buffered(1)),
                  pl.BlockSpec((1, d), lambda i: (0, 0)),
                  pl.BlockSpec((1, N_MOD, d), lambda i: (i // per, 0, 0))],
        out_specs=[row, row],
        out_shape=[jax.ShapeDtypeStruct((m, d), F32), jax.ShapeDtypeStruct((m, d), BF16)],
        compiler_params=_params("parallel"),
        name="oproj",
    )(merged, x2, w_o, g2, mod3)


def _down_body(g_ref, x1_ref, w_ref, gf_ref, mod_ref, o_ref):
    gate = mod_ref[0, 5:6, :]
    x2 = x1_ref[...] + gate * jnp.dot(g_ref[...], w_ref[...], preferred_element_type=F32)
    o_ref[...] = _rms_rows(x2) * gf_ref[...]


def _down(gff, x1, w_down, g_final, mod3, seq, bm=256):
    m, d = x1.shape
    dff = gff.shape[1]
    per = seq // bm
    row = pl.BlockSpec((bm, d), lambda i: (i, 0))
    return pl.pallas_call(
        _down_body,
        grid=(m // bm,),
        in_specs=[pl.BlockSpec((bm, dff), lambda i: (i, 0)), row,
                  pl.BlockSpec((dff, d), lambda i: (0, 0), pipeline_mode=pl.Buffered(1)),
                  pl.BlockSpec((1, d), lambda i: (0, 0)),
                  pl.BlockSpec((1, N_MOD, d), lambda i: (i // per, 0, 0))],
        out_specs=row,
        out_shape=jax.ShapeDtypeStruct((m, d), F32),
        compiler_params=_params("parallel"),
        name="down",
    )(gff, x1, w_down, g_final, mod3)


def _layer(x, c, w_ada, b_ada, g_norm1, w_in, w_ssm_conv, b_ssm_conv, dt_bias, a_log, d_skip,
           g_ssm_norm, w_ssm_out, w_cfm_dw, b_cfm_dw, g_cfm_ln, b_cfm_ln, w_cfm_out, b_cfm_out,
           w_o, g_norm2, w_up, w_ff_conv, b_ff_conv, w_down, g_final):
    bsz, seq, d = x.shape
    d_inner = w_ssm_out.shape[0]
    heads = dt_bias.shape[0]
    xbc_dim = w_ssm_conv.shape[1]
    cfm_d = w_cfm_out.shape[0]
    d_ff = w_down.shape[0]
    row = lambda v: v.reshape(1, -1).astype(F32)

    o_xbc = d_inner
    o_dt = o_xbc + xbc_dim
    o_lin = o_dt + heads
    o_gate = o_lin + cfm_d
    o_g = o_gate + cfm_d
    w_in_b = w_in.astype(BF16)
    w_tail = w_in_b[:, o_lin:]
    w_up_b = w_up.astype(BF16)
    w_dt = jnp.pad(w_in[:, o_dt:o_lin], ((0, 0), (0, LANES - heads))).astype(BF16)
    dt_bias_pad = jnp.pad(row(dt_bias), ((0, 0), (0, LANES - heads)))
    a_log_pad = jnp.pad(row(a_log), ((0, 0), (0, LANES - heads)))
    dskip_cols = jnp.repeat(d_skip.astype(F32), SSM_HEAD_DIM).reshape(1, d_inner)

    x2 = x.reshape(bsz * seq, d)
    mod3 = _ada(c, w_ada, row(b_ada)).reshape(bsz, N_MOD, d)
    h = _norm1(x2, row(g_norm1), mod3, seq)

    zs = _seqmm_ew(h, Cols(w_in_b, 0, d_inner), None, _silu, BF16, seq, COL_BLOCK, "in_z")
    xbc = _seqmm_conv(h, [Cols(w_in_b, o_xbc, xbc_dim)], w_ssm_conv.astype(F32),
                      row(b_ssm_conv), "silu", seq, COL_BLOCK, "in_xbc")
    dt = _seqmm_ew(h, Cols(w_dt, 0, LANES), dt_bias_pad, _softplus, F32, seq, LANES, "in_dt")
    uc = _seqmm_conv(h, [Cols(w_tail, 0, cfm_d), Cols(w_tail, cfm_d, cfm_d)],
                     w_cfm_dw.astype(F32), row(b_cfm_dw), "glu", seq, COL_BLOCK, "in_cfm")
    gates = _seqmm_ew(h, Cols(w_tail, 2 * cfm_d, w_in.shape[1] - o_g), None, _sigmoid, BF16,
                      seq, COL_BLOCK, "in_gates")

    yn = _ssd(xbc, dt, zs, a_log_pad, dskip_cols, row(g_ssm_norm), seq)
    merged = _merge(yn, uc, gates, w_ssm_out.astype(BF16), w_cfm_out.astype(BF16),
                    row(g_cfm_ln), row(b_cfm_ln), row(b_cfm_out))
    x1, h2 = _oproj(merged, x2, w_o.astype(BF16), row(g_norm2), mod3, seq)

    gff = _seqmm_conv(h2, [Cols(w_up_b, 0, d_ff), Cols(w_up_b, d_ff, d_ff)],
                      w_ff_conv.astype(F32), row(b_ff_conv), "gate", seq, COL_BLOCK, "ffn_up")
    out = _down(gff, x1, w_down.astype(BF16), row(g_final), mod3, seq)
    return out.reshape(bsz, seq, d)


def kernel(x, c, w_ada, b_ada, g_norm1, w_in, w_ssm_conv, b_ssm_conv, dt_bias, a_log, d_skip, g_ssm_norm, w_ssm_out, w_cfm_dw, b_cfm_dw, g_cfm_ln, b_cfm_ln, w_cfm_out, b_cfm_out, w_o, g_norm2, w_up, w_ff_conv, b_ff_conv, w_down, g_final):
    assert w_ada.shape[0] == 1, "single-layer stack"
    return _layer(x, c, w_ada[0], b_ada[0], g_norm1[0], w_in[0], w_ssm_conv[0], b_ssm_conv[0],
                  dt_bias[0], a_log[0], d_skip[0], g_ssm_norm[0], w_ssm_out[0], w_cfm_dw[0],
                  b_cfm_dw[0], g_cfm_ln[0], b_cfm_ln[0], w_cfm_out[0], b_cfm_out[0], w_o[0],
                  g_norm2[0], w_up[0], w_ff_conv[0], b_ff_conv[0], w_down[0], g_final)
```

```python
import functools
from typing import NamedTuple

import jax
import jax.numpy as jnp
from jax import lax
from jax.experimental import pallas as pl
from jax.experimental.pallas import tpu as pltpu

F32 = jnp.float32
BF16 = jnp.bfloat16

SSM_HEAD_DIM = 64
SSM_GROUPS = 8
SSM_STATE = 128
SSM_CHUNK = 256
N_MOD = 6
RMS_EPS = 1e-6
LN_EPS = 1e-5

LANES = 128
SUBLANES = 8
CHUNK = 256
COL_BLOCK = 512
HALO = 256
SUB_ROWS = 64
PARTIALS = 4
VMEM_LIMIT = 56 * 1024 * 1024


def _params(*sem):
    return pltpu.CompilerParams(dimension_semantics=sem, vmem_limit_bytes=VMEM_LIMIT)


def _sigmoid(v):
    return 1.0 / (1.0 + jnp.exp(-v))


def _silu(v):
    return v * _sigmoid(v)


def _softplus(v):
    return jnp.maximum(v, 0.0) + jnp.log1p(jnp.exp(-jnp.abs(v)))


def _rms_rows(v):
    return v * lax.rsqrt(jnp.mean(v * v, axis=-1, keepdims=True) + RMS_EPS)


def _ada_body(c_ref, w_ref, b_ref, o_ref):
    s = _silu(c_ref[...]).astype(BF16)
    o_ref[...] = jnp.dot(s, w_ref[...].astype(BF16), preferred_element_type=F32) + b_ref[...]


def _ada(c, w, b, bn=1024):
    bsz, d = c.shape
    n = w.shape[1]
    return pl.pallas_call(
        _ada_body,
        grid=(n // bn,),
        in_specs=[pl.BlockSpec((bsz, d), lambda j: (0, 0)),
                  pl.BlockSpec((d, bn), lambda j: (0, j)),
                  pl.BlockSpec((1, bn), lambda j: (0, j))],
        out_specs=pl.BlockSpec((bsz, bn), lambda j: (0, j)),
        out_shape=jax.ShapeDtypeStruct((bsz, n), F32),
        compiler_params=_params("arbitrary"),
        name="ada",
    )(c, w, b)


def _norm1_body(x_ref, g_ref, mod_ref, o_ref):
    shift = mod_ref[0, 0:1, :]
    scale = mod_ref[0, 1:2, :]
    y = _rms_rows(x_ref[...]) * g_ref[...]
    o_ref[...] = (y * (1.0 + scale) + shift).astype(BF16)


def _norm1(x2, g, mod3, seq, bm=1024):
    m, d = x2.shape
    per = seq // bm
    return pl.pallas_call(
        _norm1_body,
        grid=(m // bm,),
        in_specs=[pl.BlockSpec((bm, d), lambda i: (i, 0)),
                  pl.BlockSpec((1, d), lambda i: (0, 0)),
                  pl.BlockSpec((1, N_MOD, d), lambda i: (i // per, 0, 0))],
        out_specs=pl.BlockSpec((bm, d), lambda i: (i, 0)),
        out_shape=jax.ShapeDtypeStruct((m, d), BF16),
        compiler_params=_params("parallel"),
        name="norm1",
    )(x2, g, mod3)


class Cols(NamedTuple):
    arr: jax.Array
    start: int
    width: int


def _seq_specs(h, ws, consts, seq, bn):
    k = h.shape[1]
    in_specs = [pl.BlockSpec((seq, k), lambda b, j: (b, 0))]
    for w in ws:
        first, rem = divmod(w.start, bn)
        assert rem == 0 and w.width % bn == 0
        in_specs.append(pl.BlockSpec((k, bn), lambda b, j, first=first: (0, first + j)))
    in_specs += [pl.BlockSpec((cst.shape[0], bn), lambda b, j: (0, j)) for cst in consts]
    return in_specs


def _seqmm_ew_body(seq, fn, has_bias, h_ref, w_ref, *refs):
    o_ref = refs[-1]
    for i in range(seq // CHUNK):
        rows = slice(i * CHUNK, (i + 1) * CHUNK)
        a = jnp.dot(h_ref[rows, :], w_ref[...], preferred_element_type=F32)
        if has_bias:
            a = a + refs[0][...]
        o_ref[rows, :] = fn(a).astype(o_ref.dtype)


def _seqmm_ew(h, w, bias, fn, out_dtype, seq, bn, name):
    m = h.shape[0]
    n = w.width
    consts = [] if bias is None else [bias]
    return pl.pallas_call(
        functools.partial(_seqmm_ew_body, seq, fn, bias is not None),
        grid=(m // seq, n // bn),
        in_specs=_seq_specs(h, [w], consts, seq, bn),
        out_specs=pl.BlockSpec((seq, bn), lambda b, j: (b, j)),
        out_shape=jax.ShapeDtypeStruct((m, n), out_dtype),
        compiler_params=_params("parallel", "arbitrary"),
        name=name,
    )(h, w.arr, *consts)


def _seqmm_conv_body(seq, width, kind, drows, h_ref, w0_ref, *refs):
    two = kind in ("glu", "gate")
    w1_ref = refs[0] if two else None
    cw_ref, cb_ref, o_ref, u_ref, y_ref = refs[two:two + 5]
    v_ref = refs[two + 5] if kind == "gate" else None
    tlen = seq // SUBLANES
    dseg = drows // SUBLANES
    ndot = seq // drows
    ncol = o_ref.shape[1] // LANES

    def dot_chunk(i):
        lhs = jnp.concatenate(
            [h_ref[s * tlen + i * dseg:s * tlen + (i + 1) * dseg, :] for s in range(SUBLANES)],
            axis=0)
        a = jnp.dot(lhs, w0_ref[...], preferred_element_type=F32)
        if kind == "glu":
            a = a * _sigmoid(jnp.dot(lhs, w1_ref[...], preferred_element_type=F32))
        for c in range(ncol):
            for s in range(SUBLANES):
                u_ref[c, pl.ds(HALO + i * drows + s, dseg, stride=SUBLANES), :] = (
                    a[s * dseg:(s + 1) * dseg, c * LANES:(c + 1) * LANES])
        if kind == "gate":
            v_ref[i * drows:(i + 1) * drows, :] = jnp.dot(lhs, w1_ref[...],
                                                         preferred_element_type=F32)

    def conv_block(c, row0, rows):
        parts = [None] * min(PARTIALS, width)
        for back in range(width):
            tap = width - 1 - back
            xs = u_ref[c, pl.ds(HALO + row0 - SUBLANES * back, rows), :]
            term = cw_ref[c, tap:tap + 1, :] * xs
            k = back % len(parts)
            parts[k] = term if parts[k] is None else parts[k] + term
        while len(parts) > 1:
            parts = [a + b for a, b in zip(parts[0::2], parts[1::2])] + parts[len(parts) & ~1:]
        y_ref[c, pl.ds(row0, rows), :] = parts[0] + cb_ref[c]

    def conv_chunk(i):
        r = i * drows
        for c in range(ncol):
            for rs in range(0, drows, SUB_ROWS):
                conv_block(c, r + rs, SUB_ROWS)
        for c in range(ncol):
            cols = slice(c * LANES, (c + 1) * LANES)
            for s in range(SUBLANES):
                y = y_ref[c, pl.ds(r + s, dseg, stride=SUBLANES), :]
                if kind == "silu":
                    y = _silu(y)
                elif kind == "gate":
                    y = _silu(y) * v_ref[r + s * dseg:r + (s + 1) * dseg, cols]
                o_ref[s * tlen + i * dseg:s * tlen + (i + 1) * dseg, cols] = y.astype(o_ref.dtype)

    dot_chunk(ndot - 1)
    first = lax.broadcasted_iota(jnp.int32, (HALO, LANES), 0) % SUBLANES == 0
    for c in range(ncol):
        tail = pltpu.roll(u_ref[c, seq:HALO + seq, :], 1, axis=0)
        u_ref[c, 0:HALO, :] = jnp.where(first, 0.0, tail)
    dot_chunk(0)
    for i in range(ndot - 2):
        dot_chunk(i + 1)
        conv_chunk(i)
    conv_chunk(ndot - 2)
    conv_chunk(ndot - 1)


def _seqmm_conv(h, ws, conv_w, conv_b, kind, seq, bn, name):
    m = h.shape[0]
    n = ws[0].width
    assert seq % CHUNK == 0 and seq // SUBLANES >= HALO // SUBLANES >= conv_w.shape[0] - 1
    scratch = [pltpu.VMEM((bn // LANES, HALO + seq, LANES), F32),
               pltpu.VMEM((bn // LANES, seq, LANES), F32)]
    if kind == "gate":
        scratch.append(pltpu.VMEM((seq, bn), F32))
    width = conv_w.shape[0]
    cw = conv_w.reshape(width, n // LANES, LANES).transpose(1, 0, 2)
    cb = conv_b.reshape(n // LANES, 1, LANES)
    col_spec = lambda rows: pl.BlockSpec((bn // LANES, rows, LANES), lambda b, j: (j, 0, 0))
    return pl.pallas_call(
        functools.partial(_seqmm_conv_body, seq, width, kind, CHUNK),
        grid=(m // seq, n // bn),
        in_specs=_seq_specs(h, ws, [], seq, bn) + [col_spec(width), col_spec(1)],
        out_specs=pl.BlockSpec((seq, bn), lambda b, j: (b, j)),
        out_shape=jax.ShapeDtypeStruct((m, n), BF16),
        scratch_shapes=scratch,
        compiler_params=_params("parallel", "arbitrary"),
        name=name,
    )(h, *[w.arr for w in ws], cw, cb)


def _ssd_body(x_ref, b_ref, c_ref, dt_ref, z_ref, alog_ref, dskip_ref, gn_ref, e_ref, o_ref,
              state_ref, ex_ref):
    q = SSM_CHUNK
    p = SSM_HEAD_DIM
    n = SSM_STATE
    heads_per_group = x_ref.shape[1] // (SSM_GROUPS * p)
    gw = heads_per_group * p

    @pl.when(pl.program_id(1) == 0)
    def _():
        state_ref[...] = jnp.zeros_like(state_ref)

    dt = dt_ref[...]
    dta = dt * (-jnp.exp(alog_ref[...]))
    row = lax.broadcasted_iota(jnp.int32, (q, q), 0)
    col = lax.broadcasted_iota(jnp.int32, (q, q), 1)
    causal = row >= col
    a_cum = jnp.dot(causal.astype(F32), dta, precision=lax.Precision.HIGHEST,
                    preferred_element_type=F32)
    a_cum_t = a_cum.T
    a_last = a_cum[q - 1:q, :]

    per_head = jnp.concatenate(
        [jnp.exp(a_cum), dt * jnp.exp(a_last - a_cum),
         jnp.broadcast_to(jnp.exp(a_last), (SUBLANES, LANES))], axis=0)
    dt_t = dt.T
    hi = per_head.astype(BF16)
    lo = (per_head - hi.astype(F32)).astype(BF16)
    ex_ref[...] = (jnp.dot(hi, e_ref[...], preferred_element_type=F32)
                   + jnp.dot(lo, e_ref[...], preferred_element_type=F32))

    lane_head = lax.broadcasted_iota(jnp.int32, (1, gw), 1) // p
    for g in range(SSM_GROUPS):
        cols = slice(g * gw, (g + 1) * gw)
        bg = b_ref[:, g * n:(g + 1) * n]
        cg = c_ref[:, g * n:(g + 1) * n]
        cb = lax.dot_general(cg, bg, (((1,), (1,)), ((), ())), preferred_element_type=F32)
        prev = state_ref[g]
        xg = x_ref[:, cols].astype(F32)
        xb = x_ref[:, cols]
        lhs, rhs = [], []
        for r in range(heads_per_group):
            hd = g * heads_per_group + r
            seg = a_cum[:, hd:hd + 1] - a_cum_t[hd:hd + 1, :]
            decay = jnp.exp(jnp.where(causal, seg, -jnp.inf))
            lhs.append((decay * cb * dt_t[hd:hd + 1, :]).astype(BF16))
            rhs.append(jnp.where(lane_head == r, xb, jnp.zeros_like(xb)))
        y = jnp.dot(jnp.concatenate(lhs, axis=1), jnp.concatenate(rhs, axis=0),
                    preferred_element_type=F32)
        y = y + jnp.dot(cg, prev.astype(BF16), preferred_element_type=F32) * ex_ref[0:q, cols]
        y = y + xg * dskip_ref[:, cols]
        xend = (xg * ex_ref[q:2 * q, cols]).astype(BF16)
        new = lax.dot_general(bg, xend, (((0,), (0,)), ((), ())), preferred_element_type=F32)
        state_ref[g] = prev * ex_ref[2 * q:2 * q + 1, cols] + new
        yz = y * z_ref[:, cols].astype(F32)
        o_ref[:, cols] = (_rms_rows(yz) * gn_ref[:, cols]).astype(o_ref.dtype)


def _ssd(xbc, dt, zs, a_log_pad, dskip_cols, g_norm, seq):
    m = xbc.shape[0]
    d_inner = zs.shape[1]
    bc_w = SSM_GROUPS * SSM_STATE
    nchunk = seq // SSM_CHUNK
    q = SSM_CHUNK
    rows = lambda b, c: b * nchunk + c
    spread = (jnp.arange(d_inner)[None, :] // SSM_HEAD_DIM == jnp.arange(LANES)[:, None]).astype(BF16)
    return pl.pallas_call(
        _ssd_body,
        grid=(m // seq, nchunk),
        in_specs=[pl.BlockSpec((q, d_inner), lambda b, c: (rows(b, c), 0)),
                  pl.BlockSpec((q, bc_w), lambda b, c: (rows(b, c), d_inner // bc_w)),
                  pl.BlockSpec((q, bc_w), lambda b, c: (rows(b, c), d_inner // bc_w + 1)),
                  pl.BlockSpec((q, LANES), lambda b, c: (rows(b, c), 0)),
                  pl.BlockSpec((q, d_inner), lambda b, c: (rows(b, c), 0)),
                  pl.BlockSpec((1, LANES), lambda b, c: (0, 0)),
                  pl.BlockSpec((1, d_inner), lambda b, c: (0, 0)),
                  pl.BlockSpec((1, d_inner), lambda b, c: (0, 0)),
                  pl.BlockSpec((LANES, d_inner), lambda b, c: (0, 0))],
        out_specs=pl.BlockSpec((q, d_inner), lambda b, c: (rows(b, c), 0)),
        out_shape=jax.ShapeDtypeStruct((m, d_inner), BF16),
        scratch_shapes=[pltpu.VMEM((SSM_GROUPS, SSM_STATE, d_inner // SSM_GROUPS), F32),
                        pltpu.VMEM((2 * q + SUBLANES, d_inner), F32)],
        compiler_params=_params("parallel", "arbitrary"),
        name="ssd",
    )(xbc, xbc, xbc, dt, zs, a_log_pad, dskip_cols, g_norm, spread)


def _merge_body(yn_ref, uc_ref, ga_ref, gb_ref, wa_ref, wb_ref, lng_ref, lnb_ref, bo_ref, o_ref):
    for r0 in range(0, o_ref.shape[0], CHUNK):
        rows = slice(r0, r0 + CHUNK)
        u = uc_ref[rows, :].astype(F32)
        mu = jnp.mean(u, axis=-1, keepdims=True)
        ctr = u - mu
        var = jnp.mean(ctr * ctr, axis=-1, keepdims=True)
        un = _silu(ctr * lax.rsqrt(var + LN_EPS) * lng_ref[...] + lnb_ref[...]).astype(BF16)
        ya = jnp.dot(yn_ref[rows, :], wa_ref[...], preferred_element_type=F32)
        yb = jnp.dot(un, wb_ref[...], preferred_element_type=F32) + bo_ref[...]
        o_ref[rows, :] = (ga_ref[rows, :].astype(F32) * ya
                          + gb_ref[rows, :].astype(F32) * yb).astype(BF16)


def _merge(yn, uc, gates, wa, wb, ln_g, ln_b, b_out, bm=512):
    m, d = yn.shape
    row = pl.BlockSpec((bm, d), lambda i: (i, 0))
    vec = pl.BlockSpec((1, d), lambda i: (0, 0))
    wspec = pl.BlockSpec((d, d), lambda i: (0, 0), pipeline_mode=pl.Buffered(1))
    return pl.pallas_call(
        _merge_body,
        grid=(m // bm,),
        in_specs=[row, row, row, pl.BlockSpec((bm, d), lambda i: (i, 1)), wspec, wspec,
                  vec, vec, vec],
        out_specs=row,
        out_shape=jax.ShapeDtypeStruct((m, d), BF16),
        compiler_params=_params("parallel"),
        name="merge",
    )(yn, uc, gates, gates, wa, wb, ln_g, ln_b, b_out)


def _oproj_body(mg_ref, x_ref, w_ref, g2_ref, mod_ref, x1_ref, h2_ref):
    gate = mod_ref[0, 2:3, :]
    shift = mod_ref[0, 3:4, :]
    scale = mod_ref[0, 4:5, :]
    for r0 in range(0, x_ref.shape[0], CHUNK):
        rows = slice(r0, r0 + CHUNK)
        x1 = x_ref[rows, :] + gate * jnp.dot(mg_ref[rows, :], w_ref[...],
                                             preferred_element_type=F32)
        x1_ref[rows, :] = x1
        h2_ref[rows, :] = (_rms_rows(x1) * g2_ref[...] * (1.0 + scale) + shift).astype(BF16)


def _oproj(merged, x2, w_o, g2, mod3, seq, bm=512):
    m, d = x2.shape
    per = seq // bm
    row = pl.BlockSpec((bm, d), lambda i: (i, 0))
    return pl.pallas_call(
        _oproj_body,
        grid=(m // bm,),
        in_specs=[row, row,
                  pl.BlockSpec((d, d), lambda i: (0, 0), pipeline_mode=pl.Buffered(1)),
                  pl.BlockSpec((1, d), lambda i: (0, 0)),
                  pl.BlockSpec((1, N_MOD, d), lambda i: (i // per, 0, 0))],
        out_specs=[row, row],
        out_shape=[jax.ShapeDtypeStruct((m, d), F32), jax.ShapeDtypeStruct((m, d), BF16)],
        compiler_params=_params("parallel"),
        name="oproj",
    )(merged, x2, w_o, g2, mod3)


def _down_body(g_ref, x1_ref, w_ref, gf_ref, mod_ref, o_ref):
    gate = mod_ref[0, 5:6, :]
    x2 = x1_ref[...] + gate * jnp.dot(g_ref[...], w_ref[...], preferred_element_type=F32)
    o_ref[...] = _rms_rows(x2) * gf_ref[...]


def _down(gff, x1, w_down, g_final, mod3, seq, bm=256):
    m, d = x1.shape
    dff = gff.shape[1]
    per = seq // bm
    row = pl.BlockSpec((bm, d), lambda i: (i, 0))
    return pl.pallas_call(
        _down_body,
        grid=(m // bm,),
        in_specs=[pl.BlockSpec((bm, dff), lambda i: (i, 0)), row,
                  pl.BlockSpec((dff, d), lambda i: (0, 0), pipeline_mode=pl.Buffered(1)),
                  pl.BlockSpec((1, d), lambda i: (0, 0)),
                  pl.BlockSpec((1, N_MOD, d), lambda i: (i // per, 0, 0))],
        out_specs=row,
        out_shape=jax.ShapeDtypeStruct((m, d), F32),
        compiler_params=_params("parallel"),
        name="down",
    )(gff, x1, w_down, g_final, mod3)


def _layer(x, c, w_ada, b_ada, g_norm1, w_in, w_ssm_conv, b_ssm_conv, dt_bias, a_log, d_skip,
           g_ssm_norm, w_ssm_out, w_cfm_dw, b_cfm_dw, g_cfm_ln, b_cfm_ln, w_cfm_out, b_cfm_out,
           w_o, g_norm2, w_up, w_ff_conv, b_ff_conv, w_down, g_final):
    bsz, seq, d = x.shape
    d_inner = w_ssm_out.shape[0]
    heads = dt_bias.shape[0]
    xbc_dim = w_ssm_conv.shape[1]
    cfm_d = w_cfm_out.shape[0]
    d_ff = w_down.shape[0]
    row = lambda v: v.reshape(1, -1).astype(F32)

    o_xbc = d_inner
    o_dt = o_xbc + xbc_dim
    o_lin = o_dt + heads
    o_gate = o_lin + cfm_d
    o_g = o_gate + cfm_d
    w_in_b = w_in.astype(BF16)
    w_tail = w_in_b[:, o_lin:]
    w_up_b = w_up.astype(BF16)
    w_dt = jnp.pad(w_in[:, o_dt:o_lin], ((0, 0), (0, LANES - heads))).astype(BF16)
    dt_bias_pad = jnp.pad(row(dt_bias), ((0, 0), (0, LANES - heads)))
    a_log_pad = jnp.pad(row(a_log), ((0, 0), (0, LANES - heads)))
    dskip_cols = jnp.repeat(d_skip.astype(F32), SSM_HEAD_DIM).reshape(1, d_inner)

    x2 = x.reshape(bsz * seq, d)
    mod3 = _ada(c, w_ada, row(b_ada)).reshape(bsz, N_MOD, d)
    h = _norm1(x2, row(g_norm1), mod3, seq)

    zs = _seqmm_ew(h, Cols(w_in_b, 0, d_inner), None, _silu, BF16, seq, COL_BLOCK, "in_z")
    xbc = _seqmm_conv(h, [Cols(w_in_b, o_xbc, xbc_dim)], w_ssm_conv.astype(F32),
                      row(b_ssm_conv), "silu", seq, COL_BLOCK, "in_xbc")
    dt = _seqmm_ew(h, Cols(w_dt, 0, LANES), dt_bias_pad, _softplus, F32, seq, LANES, "in_dt")
    uc = _seqmm_conv(h, [Cols(w_tail, 0, cfm_d), Cols(w_tail, cfm_d, cfm_d)],
                     w_cfm_dw.astype(F32), row(b_cfm_dw), "glu", seq, COL_BLOCK, "in_cfm")
    gates = _seqmm_ew(h, Cols(w_tail, 2 * cfm_d, w_in.shape[1] - o_g), None, _sigmoid, BF16,
                      seq, COL_BLOCK, "in_gates")

    yn = _ssd(xbc, dt, zs, a_log_pad, dskip_cols, row(g_ssm_norm), seq)
    merged = _merge(yn, uc, gates, w_ssm_out.astype(BF16), w_cfm_out.astype(BF16),
                    row(g_cfm_ln), row(b_cfm_ln), row(b_cfm_out))
    x1, h2 = _oproj(merged, x2, w_o.astype(BF16), row(g_norm2), mod3, seq)

    gff = _seqmm_conv(h2, [Cols(w_up_b, 0, d_ff), Cols(w_up_b, d_ff, d_ff)],
                      w_ff_conv.astype(F32), row(b_ff_conv), "gate", seq, COL_BLOCK, "ffn_up")
    out = _down(gff, x1, w_down.astype(BF16), row(g_final), mod3, seq)
    return out.reshape(bsz, seq, d)


def kernel(x, c, w_ada, b_ada, g_norm1, w_in, w_ssm_conv, b_ssm_conv, dt_bias, a_log, d_skip, g_ssm_norm, w_ssm_out, w_cfm_dw, b_cfm_dw, g_cfm_ln, b_cfm_ln, w_cfm_out, b_cfm_out, w_o, g_norm2, w_up, w_ff_conv, b_ff_conv, w_down, g_final):
    assert w_ada.shape[0] == 1, "single-layer stack"
    return _layer(x, c, w_ada[0], b_ada[0], g_norm1[0], w_in[0], w_ssm_conv[0], b_ssm_conv[0],
                  dt_bias[0], a_log[0], d_skip[0], g_ssm_norm[0], w_ssm_out[0], w_cfm_dw[0],
                  b_cfm_dw[0], g_cfm_ln[0], b_cfm_ln[0], w_cfm_out[0], b_cfm_out[0], w_o[0],
                  g_norm2[0], w_up[0], w_ff_conv[0], b_ff_conv[0], w_down[0], g_final)
```

```python
import functools
from typing import NamedTuple

import jax
import jax.numpy as jnp
from jax import lax
from jax.experimental import pallas as pl
from jax.experimental.pallas import tpu as pltpu

F32 = jnp.float32
BF16 = jnp.bfloat16

SSM_HEAD_DIM = 64
SSM_GROUPS = 8
SSM_STATE = 128
SSM_CHUNK = 256
N_MOD = 6
RMS_EPS = 1e-6
LN_EPS = 1e-5

LANES = 128
SUBLANES = 8
CHUNK = 256
COL_BLOCK = 512
HALO = 256
SUB_ROWS = 64
PARTIALS = 4
VMEM_LIMIT = 56 * 1024 * 1024


def _params(*sem):
    return pltpu.CompilerParams(dimension_semantics=sem, vmem_limit_bytes=VMEM_LIMIT)


def _sigmoid(v):
    return 1.0 / (1.0 + jnp.exp(-v))


def _silu(v):
    return v * _sigmoid(v)


def _softplus(v):
    return jnp.maximum(v, 0.0) + jnp.log1p(jnp.exp(-jnp.abs(v)))


def _rms_rows(v):
    return v * lax.rsqrt(jnp.mean(v * v, axis=-1, keepdims=True) + RMS_EPS)


def _ada_body(c_ref, w_ref, b_ref, o_ref):
    s = _silu(c_ref[...]).astype(BF16)
    o_ref[...] = jnp.dot(s, w_ref[...].astype(BF16), preferred_element_type=F32) + b_ref[...]


def _ada(c, w, b, bn=1024):
    bsz, d = c.shape
    n = w.shape[1]
    return pl.pallas_call(
        _ada_body,
        grid=(n // bn,),
        in_specs=[pl.BlockSpec((bsz, d), lambda j: (0, 0)),
                  pl.BlockSpec((d, bn), lambda j: (0, j)),
                  pl.BlockSpec((1, bn), lambda j: (0, j))],
        out_specs=pl.BlockSpec((bsz, bn), lambda j: (0, j)),
        out_shape=jax.ShapeDtypeStruct((bsz, n), F32),
        compiler_params=_params("arbitrary"),
        name="ada",
    )(c, w, b)


def _norm1_body(x_ref, g_ref, mod_ref, o_ref):
    shift = mod_ref[0, 0:1, :]
    scale = mod_ref[0, 1:2, :]
    y = _rms_rows(x_ref[...]) * g_ref[...]
    o_ref[...] = (y * (1.0 + scale) + shift).astype(BF16)


def _norm1(x2, g, mod3, seq, bm=1024):
    m, d = x2.shape
    per = seq // bm
    return pl.pallas_call(
        _norm1_body,
        grid=(m // bm,),
        in_specs=[pl.BlockSpec((bm, d), lambda i: (i, 0)),
                  pl.BlockSpec((1, d), lambda i: (0, 0)),
                  pl.BlockSpec((1, N_MOD, d), lambda i: (i // per, 0, 0))],
        out_specs=pl.BlockSpec((bm, d), lambda i: (i, 0)),
        out_shape=jax.ShapeDtypeStruct((m, d), BF16),
        compiler_params=_params("parallel"),
        name="norm1",
    )(x2, g, mod3)


class Cols(NamedTuple):
    arr: jax.Array
    start: int
    width: int


def _seq_specs(h, ws, consts, seq, bn):
    k = h.shape[1]
    in_specs = [pl.BlockSpec((seq, k), lambda b, j: (b, 0))]
    for w in ws:
        first, rem = divmod(w.start, bn)
        assert rem == 0 and w.width % bn == 0
        in_specs.append(pl.BlockSpec((k, bn), lambda b, j, first=first: (0, first + j)))
    in_specs += [pl.BlockSpec((cst.shape[0], bn), lambda b, j: (0, j)) for cst in consts]
    return in_specs


def _seqmm_ew_body(seq, fn, has_bias, h_ref, w_ref, *refs):
    o_ref = refs[-1]
    for i in range(seq // CHUNK):
        rows = slice(i * CHUNK, (i + 1) * CHUNK)
        a = jnp.dot(h_ref[rows, :], w_ref[...], preferred_element_type=F32)
        if has_bias:
            a = a + refs[0][...]
        o_ref[rows, :] = fn(a).astype(o_ref.dtype)


def _seqmm_ew(h, w, bias, fn, out_dtype, seq, bn, name):
    m = h.shape[0]
    n = w.width
    consts = [] if bias is None else [bias]
    return pl.pallas_call(
        functools.partial(_seqmm_ew_body, seq, fn, bias is not None),
        grid=(m // seq, n // bn),
        in_specs=_seq_specs(h, [w], consts, seq, bn),
        out_specs=pl.BlockSpec((seq, bn), lambda b, j: (b, j)),
        out_shape=jax.ShapeDtypeStruct((m, n), out_dtype),
        compiler_params=_params("parallel", "arbitrary"),
        name=name,
    )(h, w.arr, *consts)


def _seqmm_conv_body(seq, width, kind, drows, rider_fn, h_ref, w0_ref, *refs):
    refs = list(refs)
    w1_ref = refs.pop(0) if kind in ("glu", "gate") else None
    wr_ref = refs.pop(0) if rider_fn is not None else None
    cw_ref, cb_ref, o_ref = refs[:3]
    refs = refs[3:]
    r_ref = refs.pop(0) if rider_fn is not None else None
    u_ref, y_ref = refs[:2]
    v_ref = refs[2] if kind == "gate" else None
    tlen = seq // SUBLANES
    dseg = drows // SUBLANES
    ndot = seq // drows
    ncol = o_ref.shape[1] // LANES

    def dot_chunk(i):
        lhs = jnp.concatenate(
            [h_ref[s * tlen + i * dseg:s * tlen + (i + 1) * dseg, :] for s in range(SUBLANES)],
            axis=0)
        a = jnp.dot(lhs, w0_ref[...], preferred_element_type=F32)
        if kind == "glu":
            a = a * _sigmoid(jnp.dot(lhs, w1_ref[...], preferred_element_type=F32))
        for c in range(ncol):
            for s in range(SUBLANES):
                u_ref[c, pl.ds(HALO + i * drows + s, dseg, stride=SUBLANES), :] = (
                    a[s * dseg:(s + 1) * dseg, c * LANES:(c + 1) * LANES])
        if kind == "gate":
            v_ref[i * drows:(i + 1) * drows, :] = jnp.dot(lhs, w1_ref[...],
                                                         preferred_element_type=F32)
        if rider_fn is not None:
            ra = rider_fn(jnp.dot(lhs, wr_ref[...], preferred_element_type=F32))
            for s in range(SUBLANES):
                r_ref[s * tlen + i * dseg:s * tlen + (i + 1) * dseg, :] = (
                    ra[s * dseg:(s + 1) * dseg, :].astype(r_ref.dtype))

    def conv_block(c, row0, rows):
        parts = [None] * min(PARTIALS, width)
        for back in range(width):
            tap = width - 1 - back
            xs = u_ref[c, pl.ds(HALO + row0 - SUBLANES * back, rows), :]
            term = cw_ref[c, tap:tap + 1, :] * xs
            k = back % len(parts)
            parts[k] = term if parts[k] is None else parts[k] + term
        while len(parts) > 1:
            parts = [a + b for a, b in zip(parts[0::2], parts[1::2])] + parts[len(parts) & ~1:]
        y_ref[c, pl.ds(row0, rows), :] = parts[0] + cb_ref[c]

    def conv_chunk(i):
        r = i * drows
        for c in range(ncol):
            for rs in range(0, drows, SUB_ROWS):
                conv_block(c, r + rs, SUB_ROWS)
        for c in range(ncol):
            cols = slice(c * LANES, (c + 1) * LANES)
            for s in range(SUBLANES):
                y = y_ref[c, pl.ds(r + s, dseg, stride=SUBLANES), :]
                if kind == "silu":
                    y = _silu(y)
                elif kind == "gate":
                    y = _silu(y) * v_ref[r + s * dseg:r + (s + 1) * dseg, cols]
                o_ref[s * tlen + i * dseg:s * tlen + (i + 1) * dseg, cols] = y.astype(o_ref.dtype)

    dot_chunk(ndot - 1)
    first = lax.broadcasted_iota(jnp.int32, (HALO, LANES), 0) % SUBLANES == 0
    for c in range(ncol):
        tail = pltpu.roll(u_ref[c, seq:HALO + seq, :], 1, axis=0)
        u_ref[c, 0:HALO, :] = jnp.where(first, 0.0, tail)
    dot_chunk(0)
    for i in range(ndot - 2):
        dot_chunk(i + 1)
        conv_chunk(i)
    conv_chunk(ndot - 2)
    conv_chunk(ndot - 1)


def _seqmm_conv(h, ws, conv_w, conv_b, kind, seq, bn, name, rider=None):
    m = h.shape[0]
    k = h.shape[1]
    n = ws[0].width
    assert seq % CHUNK == 0 and seq // SUBLANES >= HALO // SUBLANES >= conv_w.shape[0] - 1
    steps = n // bn
    w_specs = _seq_specs(h, ws, [], seq, bn)
    w_args = [w.arr for w in ws]
    out_specs = [pl.BlockSpec((seq, bn), lambda b, j: (b, j))]
    out_shape = [jax.ShapeDtypeStruct((m, n), BF16)]
    rider_fn = None
    if rider is not None:
        rw, rider_fn = rider
        rbn, rem = divmod(rw.width, steps)
        first, rem2 = divmod(rw.start, rbn)
        assert rem == 0 and rem2 == 0 and rbn % LANES == 0
        w_specs.append(pl.BlockSpec((k, rbn), lambda b, j, first=first: (0, first + j)))
        w_args.append(rw.arr)
        out_specs.append(pl.BlockSpec((seq, rbn), lambda b, j: (b, j)))
        out_shape.append(jax.ShapeDtypeStruct((m, rw.width), BF16))
    scratch = [pltpu.VMEM((bn // LANES, HALO + seq, LANES), F32),
               pltpu.VMEM((bn // LANES, seq, LANES), F32)]
    if kind == "gate":
        scratch.append(pltpu.VMEM((seq, bn), F32))
    width = conv_w.shape[0]
    cw = conv_w.reshape(width, n // LANES, LANES).transpose(1, 0, 2)
    cb = conv_b.reshape(n // LANES, 1, LANES)
    col_spec = lambda rows: pl.BlockSpec((bn // LANES, rows, LANES), lambda b, j: (j, 0, 0))
    outs = pl.pallas_call(
        functools.partial(_seqmm_conv_body, seq, width, kind, CHUNK, rider_fn),
        grid=(m // seq, steps),
        in_specs=w_specs + [col_spec(width), col_spec(1)],
        out_specs=out_specs,
        out_shape=out_shape,
        scratch_shapes=scratch,
        compiler_params=_params("parallel", "arbitrary"),
        name=name,
    )(h, *w_args, cw, cb)
    return outs if rider is not None else outs[0]


def _ssd_body(x_ref, b_ref, c_ref, dt_ref, z_ref, alog_ref, dskip_ref, gn_ref, e_ref, o_ref,
              state_ref, ex_ref):
    q = SSM_CHUNK
    p = SSM_HEAD_DIM
    n = SSM_STATE
    heads_per_group = x_ref.shape[1] // (SSM_GROUPS * p)
    gw = heads_per_group * p

    @pl.when(pl.program_id(1) == 0)
    def _():
        state_ref[...] = jnp.zeros_like(state_ref)

    dt = dt_ref[...]
    dta = dt * (-jnp.exp(alog_ref[...]))
    row = lax.broadcasted_iota(jnp.int32, (q, q), 0)
    col = lax.broadcasted_iota(jnp.int32, (q, q), 1)
    causal = row >= col
    a_cum = jnp.dot(causal.astype(F32), dta, precision=lax.Precision.HIGHEST,
                    preferred_element_type=F32)
    a_cum_t = a_cum.T
    a_last = a_cum[q - 1:q, :]

    per_head = jnp.concatenate(
        [jnp.exp(a_cum), dt * jnp.exp(a_last - a_cum),
         jnp.broadcast_to(jnp.exp(a_last), (SUBLANES, LANES))], axis=0)
    dt_t = dt.T
    hi = per_head.astype(BF16)
    lo = (per_head - hi.astype(F32)).astype(BF16)
    ex_ref[...] = (jnp.dot(hi, e_ref[...], preferred_element_type=F32)
                   + jnp.dot(lo, e_ref[...], preferred_element_type=F32))

    lane_head = lax.broadcasted_iota(jnp.int32, (1, gw), 1) // p
    for g in range(SSM_GROUPS):
        cols = slice(g * gw, (g + 1) * gw)
        bg = b_ref[:, g * n:(g + 1) * n]
        cg = c_ref[:, g * n:(g + 1) * n]
        cb = lax.dot_general(cg, bg, (((1,), (1,)), ((), ())), preferred_element_type=F32)
        prev = state_ref[g]
        xg = x_ref[:, cols].astype(F32)
        xb = x_ref[:, cols]
        lhs, rhs = [], []
        for r in range(heads_per_group):
            hd = g * heads_per_group + r

            def quadrant(rows, srcs, masked):
                seg = a_cum[rows, hd:hd + 1] - a_cum_t[hd:hd + 1, srcs]
                if masked:
                    seg = jnp.where(causal[rows, srcs], seg, -jnp.inf)
                return (jnp.exp(seg) * cb[rows, srcs] * dt_t[hd:hd + 1, srcs]).astype(BF16)

            early, late = slice(0, q // 2), slice(q // 2, q)
            left = jnp.concatenate([quadrant(early, early, True),
                                    quadrant(late, early, False)], axis=0)
            right = jnp.concatenate([jnp.zeros((q // 2, q // 2), BF16),
                                     quadrant(late, late, True)], axis=0)
            lhs.append(jnp.concatenate([left, right], axis=1))
            rhs.append(jnp.where(lane_head == r, xb, jnp.zeros_like(xb)))
        y = jnp.dot(jnp.concatenate(lhs, axis=1), jnp.concatenate(rhs, axis=0),
                    preferred_element_type=F32)
        y = y + jnp.dot(cg, prev.astype(BF16), preferred_element_type=F32) * ex_ref[0:q, cols]
        y = y + xg * dskip_ref[:, cols]
        xend = (xg * ex_ref[q:2 * q, cols]).astype(BF16)
        new = lax.dot_general(bg, xend, (((0,), (0,)), ((), ())), preferred_element_type=F32)
        state_ref[g] = prev * ex_ref[2 * q:2 * q + 1, cols] + new
        yz = y * z_ref[:, cols].astype(F32)
        o_ref[:, cols] = (_rms_rows(yz) * gn_ref[:, cols]).astype(o_ref.dtype)


def _ssd(xbc, dt, zs, a_log_pad, dskip_cols, g_norm, seq):
    m = xbc.shape[0]
    d_inner = zs.shape[1]
    bc_w = SSM_GROUPS * SSM_STATE
    nchunk = seq // SSM_CHUNK
    q = SSM_CHUNK
    rows = lambda b, c: b * nchunk + c
    spread = (jnp.arange(d_inner)[None, :] // SSM_HEAD_DIM == jnp.arange(LANES)[:, None]).astype(BF16)
    return pl.pallas_call(
        _ssd_body,
        grid=(m // seq, nchunk),
        in_specs=[pl.BlockSpec((q, d_inner), lambda b, c: (rows(b, c), 0)),
                  pl.BlockSpec((q, bc_w), lambda b, c: (rows(b, c), d_inner // bc_w)),
                  pl.BlockSpec((q, bc_w), lambda b, c: (rows(b, c), d_inner // bc_w + 1)),
                  pl.BlockSpec((q, LANES), lambda b, c: (rows(b, c), 0)),
                  pl.BlockSpec((q, d_inner), lambda b, c: (rows(b, c), 0)),
                  pl.BlockSpec((1, LANES), lambda b, c: (0, 0)),
                  pl.BlockSpec((1, d_inner), lambda b, c: (0, 0)),
                  pl.BlockSpec((1, d_inner), lambda b, c: (0, 0)),
                  pl.BlockSpec((LANES, d_inner), lambda b, c: (0, 0))],
        out_specs=pl.BlockSpec((q, d_inner), lambda b, c: (rows(b, c), 0)),
        out_shape=jax.ShapeDtypeStruct((m, d_inner), BF16),
        scratch_shapes=[pltpu.VMEM((SSM_GROUPS, SSM_STATE, d_inner // SSM_GROUPS), F32),
                        pltpu.VMEM((2 * q + SUBLANES, d_inner), F32)],
        compiler_params=_params("parallel", "arbitrary"),
        name="ssd",
    )(xbc, xbc, xbc, dt, zs, a_log_pad, dskip_cols, g_norm, spread)


def _merge_body(yn_ref, uc_ref, ga_ref, gb_ref, wa_ref, wb_ref, lng_ref, lnb_ref, bo_ref, o_ref):
    for r0 in range(0, o_ref.shape[0], CHUNK):
        rows = slice(r0, r0 + CHUNK)
        u = uc_ref[rows, :].astype(F32)
        mu = jnp.mean(u, axis=-1, keepdims=True)
        ctr = u - mu
        var = jnp.mean(ctr * ctr, axis=-1, keepdims=True)
        un = _silu(ctr * lax.rsqrt(var + LN_EPS) * lng_ref[...] + lnb_ref[...]).astype(BF16)
        ya = jnp.dot(yn_ref[rows, :], wa_ref[...], preferred_element_type=F32)
        yb = jnp.dot(un, wb_ref[...], preferred_element_type=F32) + bo_ref[...]
        o_ref[rows, :] = (ga_ref[rows, :].astype(F32) * ya
                          + gb_ref[rows, :].astype(F32) * yb).astype(BF16)


def _merge(yn, uc, gates, wa, wb, ln_g, ln_b, b_out, bm=512):
    m, d = yn.shape
    row = pl.BlockSpec((bm, d), lambda i: (i, 0))
    vec = pl.BlockSpec((1, d), lambda i: (0, 0))
    wspec = pl.BlockSpec((d, d), lambda i: (0, 0), pipeline_mode=pl.Buffered(1))
    return pl.pallas_call(
        _merge_body,
        grid=(m // bm,),
        in_specs=[row, row, row, pl.BlockSpec((bm, d), lambda i: (i, 1)), wspec, wspec,
                  vec, vec, vec],
        out_specs=row,
        out_shape=jax.ShapeDtypeStruct((m, d), BF16),
        compiler_params=_params("parallel"),
        name="merge",
    )(yn, uc, gates, gates, wa, wb, ln_g, ln_b, b_out)


def _oproj_body(mg_ref, x_ref, w_ref, g2_ref, mod_ref, x1_ref, h2_ref):
    gate = mod_ref[0, 2:3, :]
    shift = mod_ref[0, 3:4, :]
    scale = mod_ref[0, 4:5, :]
    for r0 in range(0, x_ref.shape[0], CHUNK):
        rows = slice(r0, r0 + CHUNK)
        x1 = x_ref[rows, :] + gate * jnp.dot(mg_ref[rows, :], w_ref[...],
                                             preferred_element_type=F32)
        x1_ref[rows, :] = x1
        h2_ref[rows, :] = (_rms_rows(x1) * g2_ref[...] * (1.0 + scale) + shift).astype(BF16)


def _oproj(merged, x2, w_o, g2, mod3, seq, bm=512):
    m, d = x2.shape
    per = seq // bm
    row = pl.BlockSpec((bm, d), lambda i: (i, 0))
    return pl.pallas_call(
        _oproj_body,
        grid=(m // bm,),
        in_specs=[row, row,
                  pl.BlockSpec((d, d), lambda i: (0, 0), pipeline_mode=pl.Buffered(1)),
                  pl.BlockSpec((1, d), lambda i: (0, 0)),
                  pl.BlockSpec((1, N_MOD, d), lambda i: (i // per, 0, 0))],
        out_specs=[row, row],
        out_shape=[jax.ShapeDtypeStruct((m, d), F32), jax.ShapeDtypeStruct((m, d), BF16)],
        compiler_params=_params("parallel"),
        name="oproj",
    )(merged, x2, w_o, g2, mod3)


def _down_body(g_ref, x1_ref, w_ref, gf_ref, mod_ref, o_ref):
    gate = mod_ref[0, 5:6, :]
    x2 = x1_ref[...] + gate * jnp.dot(g_ref[...], w_ref[...], preferred_element_type=F32)
    o_ref[...] = _rms_rows(x2) * gf_ref[...]


def _down(gff, x1, w_down, g_final, mod3, seq, bm=256):
    m, d = x1.shape
    dff = gff.shape[1]
    per = seq // bm
    row = pl.BlockSpec((bm, d), lambda i: (i, 0))
    return pl.pallas_call(
        _down_body,
        grid=(m // bm,),
        in_specs=[pl.BlockSpec((bm, dff), lambda i: (i, 0)), row,
                  pl.BlockSpec((dff, d), lambda i: (0, 0), pipeline_mode=pl.Buffered(1)),
                  pl.BlockSpec((1, d), lambda i: (0, 0)),
                  pl.BlockSpec((1, N_MOD, d), lambda i: (i // per, 0, 0))],
        out_specs=row,
        out_shape=jax.ShapeDtypeStruct((m, d), F32),
        compiler_params=_params("parallel"),
        name="down",
    )(gff, x1, w_down, g_final, mod3)


def _layer(x, c, w_ada, b_ada, g_norm1, w_in, w_ssm_conv, b_ssm_conv, dt_bias, a_log, d_skip,
           g_ssm_norm, w_ssm_out, w_cfm_dw, b_cfm_dw, g_cfm_ln, b_cfm_ln, w_cfm_out, b_cfm_out,
           w_o, g_norm2, w_up, w_ff_conv, b_ff_conv, w_down, g_final):
    bsz, seq, d = x.shape
    d_inner = w_ssm_out.shape[0]
    heads = dt_bias.shape[0]
    xbc_dim = w_ssm_conv.shape[1]
    cfm_d = w_cfm_out.shape[0]
    d_ff = w_down.shape[0]
    row = lambda v: v.reshape(1, -1).astype(F32)

    o_xbc = d_inner
    o_dt = o_xbc + xbc_dim
    o_lin = o_dt + heads
    o_gate = o_lin + cfm_d
    o_g = o_gate + cfm_d
    w_in_b = w_in.astype(BF16)
    w_tail = w_in_b[:, o_lin:]
    w_up_b = w_up.astype(BF16)
    w_dt = jnp.pad(w_in[:, o_dt:o_lin], ((0, 0), (0, LANES - heads))).astype(BF16)
    dt_bias_pad = jnp.pad(row(dt_bias), ((0, 0), (0, LANES - heads)))
    a_log_pad = jnp.pad(row(a_log), ((0, 0), (0, LANES - heads)))
    dskip_cols = jnp.repeat(d_skip.astype(F32), SSM_HEAD_DIM).reshape(1, d_inner)

    x2 = x.reshape(bsz * seq, d)
    mod3 = _ada(c, w_ada, row(b_ada)).reshape(bsz, N_MOD, d)
    h = _norm1(x2, row(g_norm1), mod3, seq)

    xbc, zs = _seqmm_conv(h, [Cols(w_in_b, o_xbc, xbc_dim)], w_ssm_conv.astype(F32),
                          row(b_ssm_conv), "silu", seq, COL_BLOCK, "in_xbc_z",
                          rider=(Cols(w_in_b, 0, d_inner), _silu))
    dt = _seqmm_ew(h, Cols(w_dt, 0, LANES), dt_bias_pad, _softplus, F32, seq, LANES, "in_dt")
    uc, gates = _seqmm_conv(h, [Cols(w_tail, 0, cfm_d), Cols(w_tail, cfm_d, cfm_d)],
                            w_cfm_dw.astype(F32), row(b_cfm_dw), "glu", seq, COL_BLOCK // 2,
                            "in_cfm_gates",
                            rider=(Cols(w_tail, 2 * cfm_d, w_in.shape[1] - o_g), _sigmoid))

    yn = _ssd(xbc, dt, zs, a_log_pad, dskip_cols, row(g_ssm_norm), seq)
    merged = _merge(yn, uc, gates, w_ssm_out.astype(BF16), w_cfm_out.astype(BF16),
                    row(g_cfm_ln), row(b_cfm_ln), row(b_cfm_out))
    x1, h2 = _oproj(merged, x2, w_o.astype(BF16), row(g_norm2), mod3, seq)

    gff = _seqmm_conv(h2, [Cols(w_up_b, 0, d_ff), Cols(w_up_b, d_ff, d_ff)],
                      w_ff_conv.astype(F32), row(b_ff_conv), "gate", seq, COL_BLOCK, "ffn_up")
    out = _down(gff, x1, w_down.astype(BF16), row(g_final), mod3, seq)
    return out.reshape(bsz, seq, d)


def kernel(x, c, w_ada, b_ada, g_norm1, w_in, w_ssm_conv, b_ssm_conv, dt_bias, a_log, d_skip, g_ssm_norm, w_ssm_out, w_cfm_dw, b_cfm_dw, g_cfm_ln, b_cfm_ln, w_cfm_out, b_cfm_out, w_o, g_norm2, w_up, w_ff_conv, b_ff_conv, w_down, g_final):
    assert w_ada.shape[0] == 1, "single-layer stack"
    return _layer(x, c, w_ada[0], b_ada[0], g_norm1[0], w_in[0], w_ssm_conv[0], b_ssm_conv[0],
                  dt_bias[0], a_log[0], d_skip[0], g_ssm_norm[0], w_ssm_out[0], w_cfm_dw[0],
                  b_cfm_dw[0], g_cfm_ln[0], b_cfm_ln[0], w_cfm_out[0], b_cfm_out[0], w_o[0],
                  g_norm2[0], w_up[0], w_ff_conv[0], b_ff_conv[0], w_down[0], g_final)
```

```python
import functools
from typing import NamedTuple

import jax
import jax.numpy as jnp
from jax import lax
from jax.experimental import pallas as pl
from jax.experimental.pallas import tpu as pltpu

F32 = jnp.float32
BF16 = jnp.bfloat16

SSM_HEAD_DIM = 64
SSM_GROUPS = 8
SSM_STATE = 128
SSM_CHUNK = 256
N_MOD = 6
RMS_EPS = 1e-6
LN_EPS = 1e-5

LANES = 128
SUBLANES = 8
CHUNK = 256
COL_BLOCK = 512
HALO = 256
SUB_ROWS = 16
PARTIALS = 4
VMEM_LIMIT = 56 * 1024 * 1024


def _params(*sem):
    return pltpu.CompilerParams(dimension_semantics=sem, vmem_limit_bytes=VMEM_LIMIT)


def _sigmoid(v):
    return 1.0 / (1.0 + jnp.exp(-v))


def _silu(v):
    return v * _sigmoid(v)


def _softplus(v):
    return jnp.maximum(v, 0.0) + jnp.log1p(jnp.exp(-jnp.abs(v)))


def _rms_rows(v):
    return v * lax.rsqrt(jnp.mean(v * v, axis=-1, keepdims=True) + RMS_EPS)


def _ada_body(c_ref, w_ref, b_ref, o_ref):
    s = _silu(c_ref[...]).astype(BF16)
    o_ref[...] = jnp.dot(s, w_ref[...].astype(BF16), preferred_element_type=F32) + b_ref[...]


def _ada(c, w, b, bn=1024):
    bsz, d = c.shape
    n = w.shape[1]
    return pl.pallas_call(
        _ada_body,
        grid=(n // bn,),
        in_specs=[pl.BlockSpec((bsz, d), lambda j: (0, 0)),
                  pl.BlockSpec((d, bn), lambda j: (0, j)),
                  pl.BlockSpec((1, bn), lambda j: (0, j))],
        out_specs=pl.BlockSpec((bsz, bn), lambda j: (0, j)),
        out_shape=jax.ShapeDtypeStruct((bsz, n), F32),
        compiler_params=_params("arbitrary"),
        name="ada",
    )(c, w, b)


def _norm1_body(x_ref, g_ref, mod_ref, o_ref):
    shift = mod_ref[0, 0:1, :]
    scale = mod_ref[0, 1:2, :]
    y = _rms_rows(x_ref[...]) * g_ref[...]
    o_ref[...] = (y * (1.0 + scale) + shift).astype(BF16)


def _norm1(x2, g, mod3, seq, bm=1024):
    m, d = x2.shape
    per = seq // bm
    return pl.pallas_call(
        _norm1_body,
        grid=(m // bm,),
        in_specs=[pl.BlockSpec((bm, d), lambda i: (i, 0)),
                  pl.BlockSpec((1, d), lambda i: (0, 0)),
                  pl.BlockSpec((1, N_MOD, d), lambda i: (i // per, 0, 0))],
        out_specs=pl.BlockSpec((bm, d), lambda i: (i, 0)),
        out_shape=jax.ShapeDtypeStruct((m, d), BF16),
        compiler_params=_params("parallel"),
        name="norm1",
    )(x2, g, mod3)


class Cols(NamedTuple):
    arr: jax.Array
    start: int
    width: int


def _seq_specs(h, ws, consts, seq, bn):
    k = h.shape[1]
    in_specs = [pl.BlockSpec((seq, k), lambda b, j: (b, 0))]
    for w in ws:
        first, rem = divmod(w.start, bn)
        assert rem == 0 and w.width % bn == 0
        in_specs.append(pl.BlockSpec((k, bn), lambda b, j, first=first: (0, first + j)))
    in_specs += [pl.BlockSpec((cst.shape[0], bn), lambda b, j: (0, j)) for cst in consts]
    return in_specs


def _seqmm_ew_body(seq, fn, has_bias, h_ref, w_ref, *refs):
    o_ref = refs[-1]
    for i in range(seq // CHUNK):
        rows = slice(i * CHUNK, (i + 1) * CHUNK)
        a = jnp.dot(h_ref[rows, :], w_ref[...], preferred_element_type=F32)
        if has_bias:
            a = a + refs[0][...]
        o_ref[rows, :] = fn(a).astype(o_ref.dtype)


def _seqmm_ew(h, w, bias, fn, out_dtype, seq, bn, name):
    m = h.shape[0]
    n = w.width
    consts = [] if bias is None else [bias]
    return pl.pallas_call(
        functools.partial(_seqmm_ew_body, seq, fn, bias is not None),
        grid=(m // seq, n // bn),
        in_specs=_seq_specs(h, [w], consts, seq, bn),
        out_specs=pl.BlockSpec((seq, bn), lambda b, j: (b, j)),
        out_shape=jax.ShapeDtypeStruct((m, n), out_dtype),
        compiler_params=_params("parallel", "arbitrary"),
        name=name,
    )(h, w.arr, *consts)


def _seqmm_conv_body(seq, width, kind, drows, rider_fn, h_ref, w0_ref, *refs):
    refs = list(refs)
    w1_ref = refs.pop(0) if kind in ("glu", "gate") else None
    wr_ref = refs.pop(0) if rider_fn is not None else None
    cw_ref, cb_ref, o_ref = refs[:3]
    refs = refs[3:]
    r_ref = refs.pop(0) if rider_fn is not None else None
    u_ref, y_ref = refs[:2]
    v_ref = refs[2] if kind == "gate" else None
    tlen = seq // SUBLANES
    dseg = drows // SUBLANES
    ndot = seq // drows
    ncol = o_ref.shape[1] // LANES

    def dot_chunk(i):
        lhs = jnp.concatenate(
            [h_ref[s * tlen + i * dseg:s * tlen + (i + 1) * dseg, :] for s in range(SUBLANES)],
            axis=0)
        a = jnp.dot(lhs, w0_ref[...], preferred_element_type=F32)
        if kind == "glu":
            a = a * _sigmoid(jnp.dot(lhs, w1_ref[...], preferred_element_type=F32))
        for c in range(ncol):
            for s in range(SUBLANES):
                u_ref[c, pl.ds(HALO + i * drows + s, dseg, stride=SUBLANES), :] = (
                    a[s * dseg:(s + 1) * dseg, c * LANES:(c + 1) * LANES])
        if kind == "gate":
            v_ref[i * drows:(i + 1) * drows, :] = jnp.dot(lhs, w1_ref[...],
                                                         preferred_element_type=F32)
        if rider_fn is not None:
            ra = rider_fn(jnp.dot(lhs, wr_ref[...], preferred_element_type=F32))
            for s in range(SUBLANES):
                r_ref[s * tlen + i * dseg:s * tlen + (i + 1) * dseg, :] = (
                    ra[s * dseg:(s + 1) * dseg, :].astype(r_ref.dtype))

    def conv_block(c, row0, rows):
        parts = [None] * min(PARTIALS, width)
        for back in range(width):
            tap = width - 1 - back
            xs = u_ref[c, pl.ds(HALO + row0 - SUBLANES * back, rows), :]
            term = cw_ref[c, tap:tap + 1, :] * xs
            k = back % len(parts)
            parts[k] = term if parts[k] is None else parts[k] + term
        while len(parts) > 1:
            parts = [a + b for a, b in zip(parts[0::2], parts[1::2])] + parts[len(parts) & ~1:]
        y_ref[c, pl.ds(row0, rows), :] = parts[0] + cb_ref[c]

    def conv_chunk(i):
        r = i * drows
        for c in range(ncol):
            for rs in range(0, drows, SUB_ROWS):
                conv_block(c, r + rs, SUB_ROWS)
        for c in range(ncol):
            cols = slice(c * LANES, (c + 1) * LANES)
            for s in range(SUBLANES):
                y = y_ref[c, pl.ds(r + s, dseg, stride=SUBLANES), :]
                if kind == "silu":
                    y = _silu(y)
                elif kind == "gate":
                    y = _silu(y) * v_ref[r + s * dseg:r + (s + 1) * dseg, cols]
                o_ref[s * tlen + i * dseg:s * tlen + (i + 1) * dseg, cols] = y.astype(o_ref.dtype)

    dot_chunk(ndot - 1)
    first = lax.broadcasted_iota(jnp.int32, (HALO, LANES), 0) % SUBLANES == 0
    for c in range(ncol):
        tail = pltpu.roll(u_ref[c, seq:HALO + seq, :], 1, axis=0)
        u_ref[c, 0:HALO, :] = jnp.where(first, 0.0, tail)
    dot_chunk(0)
    for i in range(ndot - 2):
        dot_chunk(i + 1)
        conv_chunk(i)
    conv_chunk(ndot - 2)
    conv_chunk(ndot - 1)


def _seqmm_conv(h, ws, conv_w, conv_b, kind, seq, bn, name, rider=None):
    m = h.shape[0]
    k = h.shape[1]
    n = ws[0].width
    assert seq % CHUNK == 0 and seq // SUBLANES >= HALO // SUBLANES >= conv_w.shape[0] - 1
    steps = n // bn
    w_specs = _seq_specs(h, ws, [], seq, bn)
    w_args = [w.arr for w in ws]
    out_specs = [pl.BlockSpec((seq, bn), lambda b, j: (b, j))]
    out_shape = [jax.ShapeDtypeStruct((m, n), BF16)]
    rider_fn = None
    if rider is not None:
        rw, rider_fn = rider
        rbn, rem = divmod(rw.width, steps)
        first, rem2 = divmod(rw.start, rbn)
        assert rem == 0 and rem2 == 0 and rbn % LANES == 0
        w_specs.append(pl.BlockSpec((k, rbn), lambda b, j, first=first: (0, first + j)))
        w_args.append(rw.arr)
        out_specs.append(pl.BlockSpec((seq, rbn), lambda b, j: (b, j)))
        out_shape.append(jax.ShapeDtypeStruct((m, rw.width), BF16))
    scratch = [pltpu.VMEM((bn // LANES, HALO + seq, LANES), F32),
               pltpu.VMEM((bn // LANES, seq, LANES), F32)]
    if kind == "gate":
        scratch.append(pltpu.VMEM((seq, bn), F32))
    width = conv_w.shape[0]
    cw = conv_w.reshape(width, n // LANES, LANES).transpose(1, 0, 2)
    cb = conv_b.reshape(n // LANES, 1, LANES)
    col_spec = lambda rows: pl.BlockSpec((bn // LANES, rows, LANES), lambda b, j: (j, 0, 0))
    outs = pl.pallas_call(
        functools.partial(_seqmm_conv_body, seq, width, kind, CHUNK, rider_fn),
        grid=(m // seq, steps),
        in_specs=w_specs + [col_spec(width), col_spec(1)],
        out_specs=out_specs,
        out_shape=out_shape,
        scratch_shapes=scratch,
        compiler_params=_params("parallel", "arbitrary"),
        name=name,
    )(h, *w_args, cw, cb)
    return outs if rider is not None else outs[0]


def _ssd_body(x_ref, b_ref, c_ref, dt_ref, z_ref, alog_ref, dskip_ref, gn_ref, e_ref, o_ref,
              state_ref, ex_ref):
    q = SSM_CHUNK
    p = SSM_HEAD_DIM
    n = SSM_STATE
    heads_per_group = x_ref.shape[1] // (SSM_GROUPS * p)
    gw = heads_per_group * p

    @pl.when(pl.program_id(1) == 0)
    def _():
        state_ref[...] = jnp.zeros_like(state_ref)

    dt = dt_ref[...]
    dta = dt * (-jnp.exp(alog_ref[...]))
    row = lax.broadcasted_iota(jnp.int32, (q, q), 0)
    col = lax.broadcasted_iota(jnp.int32, (q, q), 1)
    causal = row >= col
    a_cum = jnp.dot(causal.astype(F32), dta, precision=lax.Precision.HIGHEST,
                    preferred_element_type=F32)
    a_cum_t = a_cum.T
    a_last = a_cum[q - 1:q, :]

    per_head = jnp.concatenate(
        [jnp.exp(a_cum), dt * jnp.exp(a_last - a_cum),
         jnp.broadcast_to(jnp.exp(a_last), (SUBLANES, LANES))], axis=0)
    dt_t = dt.T
    hi = per_head.astype(BF16)
    lo = (per_head - hi.astype(F32)).astype(BF16)
    ex_ref[...] = (jnp.dot(hi, e_ref[...], preferred_element_type=F32)
                   + jnp.dot(lo, e_ref[...], preferred_element_type=F32))

    lane_head = lax.broadcasted_iota(jnp.int32, (1, gw), 1) // p
    for g in range(SSM_GROUPS):
        cols = slice(g * gw, (g + 1) * gw)
        bg = b_ref[:, g * n:(g + 1) * n]
        cg = c_ref[:, g * n:(g + 1) * n]
        cb = lax.dot_general(cg, bg, (((1,), (1,)), ((), ())), preferred_element_type=F32)
        prev = state_ref[g]
        xg = x_ref[:, cols].astype(F32)
        xb = x_ref[:, cols]
        lhs, rhs = [], []
        for r in range(heads_per_group):
            hd = g * heads_per_group + r

            def quadrant(rows, srcs, masked):
                seg = a_cum[rows, hd:hd + 1] - a_cum_t[hd:hd + 1, srcs]
                if masked:
                    seg = jnp.where(causal[rows, srcs], seg, -jnp.inf)
                return (jnp.exp(seg) * cb[rows, srcs] * dt_t[hd:hd + 1, srcs]).astype(BF16)

            early, late = slice(0, q // 2), slice(q // 2, q)
            left = jnp.concatenate([quadrant(early, early, True),
                                    quadrant(late, early, False)], axis=0)
            right = jnp.concatenate([jnp.zeros((q // 2, q // 2), BF16),
                                     quadrant(late, late, True)], axis=0)
            lhs.append(jnp.concatenate([left, right], axis=1))
            rhs.append(jnp.where(lane_head == r, xb, jnp.zeros_like(xb)))
        y = jnp.dot(jnp.concatenate(lhs, axis=1), jnp.concatenate(rhs, axis=0),
                    preferred_element_type=F32)
        y = y + jnp.dot(cg, prev.astype(BF16), preferred_element_type=F32) * ex_ref[0:q, cols]
        y = y + xg * dskip_ref[:, cols]
        xend = (xg * ex_ref[q:2 * q, cols]).astype(BF16)
        new = lax.dot_general(bg, xend, (((0,), (0,)), ((), ())), preferred_element_type=F32)
        state_ref[g] = prev * ex_ref[2 * q:2 * q + 1, cols] + new
        yz = y * z_ref[:, cols].astype(F32)
        o_ref[:, cols] = (_rms_rows(yz) * gn_ref[:, cols]).astype(o_ref.dtype)


def _ssd(xbc, dt, zs, a_log_pad, dskip_cols, g_norm, seq):
    m = xbc.shape[0]
    d_inner = zs.shape[1]
    bc_w = SSM_GROUPS * SSM_STATE
    nchunk = seq // SSM_CHUNK
    q = SSM_CHUNK
    rows = lambda b, c: b * nchunk + c
    spread = (jnp.arange(d_inner)[None, :] // SSM_HEAD_DIM == jnp.arange(LANES)[:, None]).astype(BF16)
    return pl.pallas_call(
        _ssd_body,
        grid=(m // seq, nchunk),
        in_specs=[pl.BlockSpec((q, d_inner), lambda b, c: (rows(b, c), 0)),
                  pl.BlockSpec((q, bc_w), lambda b, c: (rows(b, c), d_inner // bc_w)),
                  pl.BlockSpec((q, bc_w), lambda b, c: (rows(b, c), d_inner // bc_w + 1)),
                  pl.BlockSpec((q, LANES), lambda b, c: (rows(b, c), 0)),
                  pl.BlockSpec((q, d_inner), lambda b, c: (rows(b, c), 0)),
                  pl.BlockSpec((1, LANES), lambda b, c: (0, 0)),
                  pl.BlockSpec((1, d_inner), lambda b, c: (0, 0)),
                  pl.BlockSpec((1, d_inner), lambda b, c: (0, 0)),
                  pl.BlockSpec((LANES, d_inner), lambda b, c: (0, 0))],
        out_specs=pl.BlockSpec((q, d_inner), lambda b, c: (rows(b, c), 0)),
        out_shape=jax.ShapeDtypeStruct((m, d_inner), BF16),
        scratch_shapes=[pltpu.VMEM((SSM_GROUPS, SSM_STATE, d_inner // SSM_GROUPS), F32),
                        pltpu.VMEM((2 * q + SUBLANES, d_inner), F32)],
        compiler_params=_params("parallel", "arbitrary"),
        name="ssd",
    )(xbc, xbc, xbc, dt, zs, a_log_pad, dskip_cols, g_norm, spread)


def _merge_body(yn_ref, uc_ref, ga_ref, gb_ref, wa_ref, wb_ref, lng_ref, lnb_ref, bo_ref, o_ref):
    for r0 in range(0, o_ref.shape[0], CHUNK):
        rows = slice(r0, r0 + CHUNK)
        u = uc_ref[rows, :].astype(F32)
        mu = jnp.mean(u, axis=-1, keepdims=True)
        ctr = u - mu
        var = jnp.mean(ctr * ctr, axis=-1, keepdims=True)
        un = _silu(ctr * lax.rsqrt(var + LN_EPS) * lng_ref[...] + lnb_ref[...]).astype(BF16)
        ya = jnp.dot(yn_ref[rows, :], wa_ref[...], preferred_element_type=F32)
        yb = jnp.dot(un, wb_ref[...], preferred_element_type=F32) + bo_ref[...]
        o_ref[rows, :] = (ga_ref[rows, :].astype(F32) * ya
                          + gb_ref[rows, :].astype(F32) * yb).astype(BF16)


def _merge(yn, uc, gates, wa, wb, ln_g, ln_b, b_out, bm=512):
    m, d = yn.shape
    row = pl.BlockSpec((bm, d), lambda i: (i, 0))
    vec = pl.BlockSpec((1, d), lambda i: (0, 0))
    wspec = pl.BlockSpec((d, d), lambda i: (0, 0), pipeline_mode=pl.Buffered(1))
    return pl.pallas_call(
        _merge_body,
        grid=(m // bm,),
        in_specs=[row, row, row, pl.BlockSpec((bm, d), lambda i: (i, 1)), wspec, wspec,
                  vec, vec, vec],
        out_specs=row,
        out_shape=jax.ShapeDtypeStruct((m, d), BF16),
        compiler_params=_params("parallel"),
        name="merge",
    )(yn, uc, gates, gates, wa, wb, ln_g, ln_b, b_out)


def _oproj_body(mg_ref, x_ref, w_ref, g2_ref, mod_ref, x1_ref, h2_ref):
    gate = mod_ref[0, 2:3, :]
    shift = mod_ref[0, 3:4, :]
    scale = mod_ref[0, 4:5, :]
    for r0 in range(0, x_ref.shape[0], CHUNK):
        rows = slice(r0, r0 + CHUNK)
        x1 = x_ref[rows, :] + gate * jnp.dot(mg_ref[rows, :], w_ref[...],
                                             preferred_element_type=F32)
        x1_ref[rows, :] = x1
        h2_ref[rows, :] = (_rms_rows(x1) * g2_ref[...] * (1.0 + scale) + shift).astype(BF16)


def _oproj(merged, x2, w_o, g2, mod3, seq, bm=512):
    m, d = x2.shape
    per = seq // bm
    row = pl.BlockSpec((bm, d), lambda i: (i, 0))
    return pl.pallas_call(
        _oproj_body,
        grid=(m // bm,),
        in_specs=[row, row,
                  pl.BlockSpec((d, d), lambda i: (0, 0), pipeline_mode=pl.Buffered(1)),
                  pl.BlockSpec((1, d), lambda i: (0, 0)),
                  pl.BlockSpec((1, N_MOD, d), lambda i: (i // per, 0, 0))],
        out_specs=[row, row],
        out_shape=[jax.ShapeDtypeStruct((m, d), F32), jax.ShapeDtypeStruct((m, d), BF16)],
        compiler_params=_params("parallel"),
        name="oproj",
    )(merged, x2, w_o, g2, mod3)


def _down_body(g_ref, x1_ref, w_ref, gf_ref, mod_ref, o_ref):
    gate = mod_ref[0, 5:6, :]
    x2 = x1_ref[...] + gate * jnp.dot(g_ref[...], w_ref[...], preferred_element_type=F32)
    o_ref[...] = _rms_rows(x2) * gf_ref[...]


def _down(gff, x1, w_down, g_final, mod3, seq, bm=256):
    m, d = x1.shape
    dff = gff.shape[1]
    per = seq // bm
    row = pl.BlockSpec((bm, d), lambda i: (i, 0))
    return pl.pallas_call(
        _down_body,
        grid=(m // bm,),
        in_specs=[pl.BlockSpec((bm, dff), lambda i: (i, 0)), row,
                  pl.BlockSpec((dff, d), lambda i: (0, 0), pipeline_mode=pl.Buffered(1)),
                  pl.BlockSpec((1, d), lambda i: (0, 0)),
                  pl.BlockSpec((1, N_MOD, d), lambda i: (i // per, 0, 0))],
        out_specs=row,
        out_shape=jax.ShapeDtypeStruct((m, d), F32),
        compiler_params=_params("parallel"),
        name="down",
    )(gff, x1, w_down, g_final, mod3)


def _layer(x, c, w_ada, b_ada, g_norm1, w_in, w_ssm_conv, b_ssm_conv, dt_bias, a_log, d_skip,
           g_ssm_norm, w_ssm_out, w_cfm_dw, b_cfm_dw, g_cfm_ln, b_cfm_ln, w_cfm_out, b_cfm_out,
           w_o, g_norm2, w_up, w_ff_conv, b_ff_conv, w_down, g_final):
    bsz, seq, d = x.shape
    d_inner = w_ssm_out.shape[0]
    heads = dt_bias.shape[0]
    xbc_dim = w_ssm_conv.shape[1]
    cfm_d = w_cfm_out.shape[0]
    d_ff = w_down.shape[0]
    row = lambda v: v.reshape(1, -1).astype(F32)

    o_xbc = d_inner
    o_dt = o_xbc + xbc_dim
    o_lin = o_dt + heads
    o_gate = o_lin + cfm_d
    o_g = o_gate + cfm_d
    w_in_b = w_in.astype(BF16)
    w_tail = w_in_b[:, o_lin:]
    w_up_b = w_up.astype(BF16)
    w_dt = jnp.pad(w_in[:, o_dt:o_lin], ((0, 0), (0, LANES - heads))).astype(BF16)
    dt_bias_pad = jnp.pad(row(dt_bias), ((0, 0), (0, LANES - heads)))
    a_log_pad = jnp.pad(row(a_log), ((0, 0), (0, LANES - heads)))
    dskip_cols = jnp.repeat(d_skip.astype(F32), SSM_HEAD_DIM).reshape(1, d_inner)

    x2 = x.reshape(bsz * seq, d)
    mod3 = _ada(c, w_ada, row(b_ada)).reshape(bsz, N_MOD, d)
    h = _norm1(x2, row(g_norm1), mod3, seq)

    xbc, zs = _seqmm_conv(h, [Cols(w_in_b, o_xbc, xbc_dim)], w_ssm_conv.astype(F32),
                          row(b_ssm_conv), "silu", seq, COL_BLOCK, "in_xbc_z",
                          rider=(Cols(w_in_b, 0, d_inner), _silu))
    dt = _seqmm_ew(h, Cols(w_dt, 0, LANES), dt_bias_pad, _softplus, F32, seq, LANES, "in_dt")
    uc, gates = _seqmm_conv(h, [Cols(w_tail, 0, cfm_d), Cols(w_tail, cfm_d, cfm_d)],
                            w_cfm_dw.astype(F32), row(b_cfm_dw), "glu", seq, COL_BLOCK // 2,
                            "in_cfm_gates",
                            rider=(Cols(w_tail, 2 * cfm_d, w_in.shape[1] - o_g), _sigmoid))

    yn = _ssd(xbc, dt, zs, a_log_pad, dskip_cols, row(g_ssm_norm), seq)
    merged = _merge(yn, uc, gates, w_ssm_out.astype(BF16), w_cfm_out.astype(BF16),
                    row(g_cfm_ln), row(b_cfm_ln), row(b_cfm_out))
    x1, h2 = _oproj(merged, x2, w_o.astype(BF16), row(g_norm2), mod3, seq)

    gff = _seqmm_conv(h2, [Cols(w_up_b, 0, d_ff), Cols(w_up_b, d_ff, d_ff)],
                      w_ff_conv.astype(F32), row(b_ff_conv), "gate", seq, COL_BLOCK, "ffn_up")
    out = _down(gff, x1, w_down.astype(BF16), row(g_final), mod3, seq)
    return out.reshape(bsz, seq, d)


def kernel(x, c, w_ada, b_ada, g_norm1, w_in, w_ssm_conv, b_ssm_conv, dt_bias, a_log, d_skip, g_ssm_norm, w_ssm_out, w_cfm_dw, b_cfm_dw, g_cfm_ln, b_cfm_ln, w_cfm_out, b_cfm_out, w_o, g_norm2, w_up, w_ff_conv, b_ff_conv, w_down, g_final):
    assert w_ada.shape[0] == 1, "single-layer stack"
    return _layer(x, c, w_ada[0], b_ada[0], g_norm1[0], w_in[0], w_ssm_conv[0], b_ssm_conv[0],
                  dt_bias[0], a_log[0], d_skip[0], g_ssm_norm[0], w_ssm_out[0], w_cfm_dw[0],
                  b_cfm_dw[0], g_cfm_ln[0], b_cfm_ln[0], w_cfm_out[0], b_cfm_out[0], w_o[0],
                  g_norm2[0], w_up[0], w_ff_conv[0], b_ff_conv[0], w_down[0], g_final)
```

```python
import functools
from typing import NamedTuple

import jax
import jax.numpy as jnp
from jax import lax
from jax.experimental import pallas as pl
from jax.experimental.pallas import tpu as pltpu

F32 = jnp.float32
BF16 = jnp.bfloat16

SSM_HEAD_DIM = 64
SSM_GROUPS = 8
SSM_STATE = 128
SSM_CHUNK = 256
N_MOD = 6
RMS_EPS = 1e-6
LN_EPS = 1e-5

LANES = 128
SUBLANES = 8
CHUNK = 256
COL_BLOCK = 512
HALO = 256
SUB_ROWS = 16
PARTIALS = 4
VMEM_LIMIT = 56 * 1024 * 1024


def _params(*sem):
    return pltpu.CompilerParams(dimension_semantics=sem, vmem_limit_bytes=VMEM_LIMIT)


def _sigmoid(v):
    return 1.0 / (1.0 + jnp.exp(-v))


def _silu(v):
    return v * _sigmoid(v)


def _softplus(v):
    return jnp.maximum(v, 0.0) + jnp.log1p(jnp.exp(-jnp.abs(v)))


def _rms_rows(v):
    return v * lax.rsqrt(jnp.mean(v * v, axis=-1, keepdims=True) + RMS_EPS)


def _ada_body(c_ref, w_ref, b_ref, o_ref):
    s = _silu(c_ref[...]).astype(BF16)
    o_ref[...] = jnp.dot(s, w_ref[...].astype(BF16), preferred_element_type=F32) + b_ref[...]


def _ada(c, w, b, bn=1024):
    bsz, d = c.shape
    n = w.shape[1]
    return pl.pallas_call(
        _ada_body,
        grid=(n // bn,),
        in_specs=[pl.BlockSpec((bsz, d), lambda j: (0, 0)),
                  pl.BlockSpec((d, bn), lambda j: (0, j)),
                  pl.BlockSpec((1, bn), lambda j: (0, j))],
        out_specs=pl.BlockSpec((bsz, bn), lambda j: (0, j)),
        out_shape=jax.ShapeDtypeStruct((bsz, n), F32),
        compiler_params=_params("arbitrary"),
        name="ada",
    )(c, w, b)


def _norm1_body(x_ref, g_ref, mod_ref, o_ref):
    shift = mod_ref[0, 0:1, :]
    scale = mod_ref[0, 1:2, :]
    y = _rms_rows(x_ref[...]) * g_ref[...]
    o_ref[...] = (y * (1.0 + scale) + shift).astype(BF16)


def _norm1(x2, g, mod3, seq, bm=1024):
    m, d = x2.shape
    per = seq // bm
    return pl.pallas_call(
        _norm1_body,
        grid=(m // bm,),
        in_specs=[pl.BlockSpec((bm, d), lambda i: (i, 0)),
                  pl.BlockSpec((1, d), lambda i: (0, 0)),
                  pl.BlockSpec((1, N_MOD, d), lambda i: (i // per, 0, 0))],
        out_specs=pl.BlockSpec((bm, d), lambda i: (i, 0)),
        out_shape=jax.ShapeDtypeStruct((m, d), BF16),
        compiler_params=_params("parallel"),
        name="norm1",
    )(x2, g, mod3)


class Cols(NamedTuple):
    arr: jax.Array
    start: int
    width: int


def _seq_specs(h, ws, consts, seq, bn):
    k = h.shape[1]
    in_specs = [pl.BlockSpec((seq, k), lambda b, j: (b, 0))]
    for w in ws:
        first, rem = divmod(w.start, bn)
        assert rem == 0 and w.width % bn == 0
        in_specs.append(pl.BlockSpec((k, bn), lambda b, j, first=first: (0, first + j)))
    in_specs += [pl.BlockSpec((cst.shape[0], bn), lambda b, j: (0, j)) for cst in consts]
    return in_specs


def _seqmm_ew_body(seq, fn, has_bias, h_ref, w_ref, *refs):
    o_ref = refs[-1]
    for i in range(seq // CHUNK):
        rows = slice(i * CHUNK, (i + 1) * CHUNK)
        a = jnp.dot(h_ref[rows, :], w_ref[...], preferred_element_type=F32)
        if has_bias:
            a = a + refs[0][...]
        o_ref[rows, :] = fn(a).astype(o_ref.dtype)


def _seqmm_ew(h, w, bias, fn, out_dtype, seq, bn, name):
    m = h.shape[0]
    n = w.width
    consts = [] if bias is None else [bias]
    return pl.pallas_call(
        functools.partial(_seqmm_ew_body, seq, fn, bias is not None),
        grid=(m // seq, n // bn),
        in_specs=_seq_specs(h, [w], consts, seq, bn),
        out_specs=pl.BlockSpec((seq, bn), lambda b, j: (b, j)),
        out_shape=jax.ShapeDtypeStruct((m, n), out_dtype),
        compiler_params=_params("parallel", "arbitrary"),
        name=name,
    )(h, w.arr, *consts)


def _seqmm_conv_body(seq, width, kind, drows, rider_fn, h_ref, w0_ref, *refs):
    refs = list(refs)
    w1_ref = refs.pop(0) if kind in ("glu", "gate") else None
    wr_ref = refs.pop(0) if rider_fn is not None else None
    cw_ref, cb_ref, o_ref = refs[:3]
    refs = refs[3:]
    r_ref = refs.pop(0) if rider_fn is not None else None
    u_ref, y_ref = refs[:2]
    v_ref = refs[2] if kind == "gate" else None
    tlen = seq // SUBLANES
    dseg = drows // SUBLANES
    ndot = seq // drows
    ncol = o_ref.shape[1] // LANES

    def dot_chunk(i):
        lhs = jnp.concatenate(
            [h_ref[s * tlen + i * dseg:s * tlen + (i + 1) * dseg, :] for s in range(SUBLANES)],
            axis=0)
        a = jnp.dot(lhs, w0_ref[...], preferred_element_type=F32)
        if kind == "glu":
            a = a * _sigmoid(jnp.dot(lhs, w1_ref[...], preferred_element_type=F32))
        for c in range(ncol):
            for s in range(SUBLANES):
                u_ref[c, pl.ds(HALO + i * drows + s, dseg, stride=SUBLANES), :] = (
                    a[s * dseg:(s + 1) * dseg, c * LANES:(c + 1) * LANES])
        if kind == "gate":
            v_ref[i * drows:(i + 1) * drows, :] = jnp.dot(lhs, w1_ref[...],
                                                         preferred_element_type=F32)
        if rider_fn is not None:
            ra = rider_fn(jnp.dot(lhs, wr_ref[...], preferred_element_type=F32))
            for s in range(SUBLANES):
                r_ref[s * tlen + i * dseg:s * tlen + (i + 1) * dseg, :] = (
                    ra[s * dseg:(s + 1) * dseg, :].astype(r_ref.dtype))

    def conv_block(c, row0, rows):
        parts = [None] * min(PARTIALS, width)
        for back in range(width):
            tap = width - 1 - back
            xs = u_ref[c, pl.ds(HALO + row0 - SUBLANES * back, rows), :]
            term = cw_ref[c, tap:tap + 1, :] * xs
            k = back % len(parts)
            parts[k] = term if parts[k] is None else parts[k] + term
        while len(parts) > 1:
            parts = [a + b for a, b in zip(parts[0::2], parts[1::2])] + parts[len(parts) & ~1:]
        y_ref[c, pl.ds(row0, rows), :] = parts[0] + cb_ref[c]

    def conv_chunk(i):
        r = i * drows
        for c in range(ncol):
            sub = SUB_ROWS if width > SUBLANES else 4 * SUB_ROWS
            for rs in range(0, drows, sub):
                conv_block(c, r + rs, sub)
        for c in range(ncol):
            cols = slice(c * LANES, (c + 1) * LANES)
            for s in range(SUBLANES):
                y = y_ref[c, pl.ds(r + s, dseg, stride=SUBLANES), :]
                if kind == "silu":
                    y = _silu(y)
                elif kind == "gate":
                    y = _silu(y) * v_ref[r + s * dseg:r + (s + 1) * dseg, cols]
                o_ref[s * tlen + i * dseg:s * tlen + (i + 1) * dseg, cols] = y.astype(o_ref.dtype)

    dot_chunk(ndot - 1)
    first = lax.broadcasted_iota(jnp.int32, (HALO, LANES), 0) % SUBLANES == 0
    for c in range(ncol):
        tail = pltpu.roll(u_ref[c, seq:HALO + seq, :], 1, axis=0)
        u_ref[c, 0:HALO, :] = jnp.where(first, 0.0, tail)
    dot_chunk(0)
    for i in range(ndot - 2):
        dot_chunk(i + 1)
        conv_chunk(i)
    conv_chunk(ndot - 2)
    conv_chunk(ndot - 1)


def _seqmm_conv(h, ws, conv_w, conv_b, kind, seq, bn, name, rider=None):
    m = h.shape[0]
    k = h.shape[1]
    n = ws[0].width
    assert seq % CHUNK == 0 and seq // SUBLANES >= HALO // SUBLANES >= conv_w.shape[0] - 1
    steps = n // bn
    w_specs = _seq_specs(h, ws, [], seq, bn)
    w_args = [w.arr for w in ws]
    out_specs = [pl.BlockSpec((seq, bn), lambda b, j: (b, j))]
    out_shape = [jax.ShapeDtypeStruct((m, n), BF16)]
    rider_fn = None
    if rider is not None:
        rw, rider_fn = rider
        rbn, rem = divmod(rw.width, steps)
        first, rem2 = divmod(rw.start, rbn)
        assert rem == 0 and rem2 == 0 and rbn % LANES == 0
        w_specs.append(pl.BlockSpec((k, rbn), lambda b, j, first=first: (0, first + j)))
        w_args.append(rw.arr)
        out_specs.append(pl.BlockSpec((seq, rbn), lambda b, j: (b, j)))
        out_shape.append(jax.ShapeDtypeStruct((m, rw.width), BF16))
    scratch = [pltpu.VMEM((bn // LANES, HALO + seq, LANES), F32),
               pltpu.VMEM((bn // LANES, seq, LANES), F32)]
    if kind == "gate":
        scratch.append(pltpu.VMEM((seq, bn), F32))
    width = conv_w.shape[0]
    cw = conv_w.reshape(width, n // LANES, LANES).transpose(1, 0, 2)
    cb = conv_b.reshape(n // LANES, 1, LANES)
    col_spec = lambda rows: pl.BlockSpec((bn // LANES, rows, LANES), lambda b, j: (j, 0, 0))
    outs = pl.pallas_call(
        functools.partial(_seqmm_conv_body, seq, width, kind, CHUNK, rider_fn),
        grid=(m // seq, steps),
        in_specs=w_specs + [col_spec(width), col_spec(1)],
        out_specs=out_specs,
        out_shape=out_shape,
        scratch_shapes=scratch,
        compiler_params=_params("parallel", "arbitrary"),
        name=name,
    )(h, *w_args, cw, cb)
    return outs if rider is not None else outs[0]


def _ssd_body(x_ref, b_ref, c_ref, dt_ref, z_ref, alog_ref, dskip_ref, gn_ref, e_ref, o_ref,
              state_ref, ex_ref):
    q = SSM_CHUNK
    p = SSM_HEAD_DIM
    n = SSM_STATE
    heads_per_group = x_ref.shape[1] // (SSM_GROUPS * p)
    gw = heads_per_group * p

    @pl.when(pl.program_id(1) == 0)
    def _():
        state_ref[...] = jnp.zeros_like(state_ref)

    dt = dt_ref[...]
    dta = dt * (-jnp.exp(alog_ref[...]))
    row = lax.broadcasted_iota(jnp.int32, (q, q), 0)
    col = lax.broadcasted_iota(jnp.int32, (q, q), 1)
    causal = row >= col
    a_cum = jnp.dot(causal.astype(F32), dta, precision=lax.Precision.HIGHEST,
                    preferred_element_type=F32)
    a_cum_t = a_cum.T
    a_last = a_cum[q - 1:q, :]

    per_head = jnp.concatenate(
        [jnp.exp(a_cum), dt * jnp.exp(a_last - a_cum),
         jnp.broadcast_to(jnp.exp(a_last), (SUBLANES, LANES))], axis=0)
    dt_t = dt.T
    hi = per_head.astype(BF16)
    lo = (per_head - hi.astype(F32)).astype(BF16)
    ex_ref[...] = (jnp.dot(hi, e_ref[...], preferred_element_type=F32)
                   + jnp.dot(lo, e_ref[...], preferred_element_type=F32))

    lane_head = lax.broadcasted_iota(jnp.int32, (1, gw), 1) // p
    for g in range(SSM_GROUPS):
        cols = slice(g * gw, (g + 1) * gw)
        bg = b_ref[:, g * n:(g + 1) * n]
        cg = c_ref[:, g * n:(g + 1) * n]
        cb = lax.dot_general(cg, bg, (((1,), (1,)), ((), ())), preferred_element_type=F32)
        prev = state_ref[g]
        xg = x_ref[:, cols].astype(F32)
        xb = x_ref[:, cols]
        lhs, rhs = [], []
        for r in range(heads_per_group):
            hd = g * heads_per_group + r

            def quadrant(rows, srcs, masked):
                seg = a_cum[rows, hd:hd + 1] - a_cum_t[hd:hd + 1, srcs]
                if masked:
                    seg = jnp.where(causal[rows, srcs], seg, -jnp.inf)
                return (jnp.exp(seg) * cb[rows, srcs] * dt_t[hd:hd + 1, srcs]).astype(BF16)

            early, late = slice(0, q // 2), slice(q // 2, q)
            left = jnp.concatenate([quadrant(early, early, True),
                                    quadrant(late, early, False)], axis=0)
            right = jnp.concatenate([jnp.zeros((q // 2, q // 2), BF16),
                                     quadrant(late, late, True)], axis=0)
            lhs.append(jnp.concatenate([left, right], axis=1))
            rhs.append(jnp.where(lane_head == r, xb, jnp.zeros_like(xb)))
        y = jnp.dot(jnp.concatenate(lhs, axis=1), jnp.concatenate(rhs, axis=0),
                    preferred_element_type=F32)
        y = y + jnp.dot(cg, prev.astype(BF16), preferred_element_type=F32) * ex_ref[0:q, cols]
        y = y + xg * dskip_ref[:, cols]
        xend = (xg * ex_ref[q:2 * q, cols]).astype(BF16)
        new = lax.dot_general(bg, xend, (((0,), (0,)), ((), ())), preferred_element_type=F32)
        state_ref[g] = prev * ex_ref[2 * q:2 * q + 1, cols] + new
        yz = y * z_ref[:, cols].astype(F32)
        o_ref[:, cols] = (_rms_rows(yz) * gn_ref[:, cols]).astype(o_ref.dtype)


def _ssd(xbc, dt, zs, a_log_pad, dskip_cols, g_norm, seq):
    m = xbc.shape[0]
    d_inner = zs.shape[1]
    bc_w = SSM_GROUPS * SSM_STATE
    nchunk = seq // SSM_CHUNK
    q = SSM_CHUNK
    rows = lambda b, c: b * nchunk + c
    spread = (jnp.arange(d_inner)[None, :] // SSM_HEAD_DIM == jnp.arange(LANES)[:, None]).astype(BF16)
    return pl.pallas_call(
        _ssd_body,
        grid=(m // seq, nchunk),
        in_specs=[pl.BlockSpec((q, d_inner), lambda b, c: (rows(b, c), 0)),
                  pl.BlockSpec((q, bc_w), lambda b, c: (rows(b, c), d_inner // bc_w)),
                  pl.BlockSpec((q, bc_w), lambda b, c: (rows(b, c), d_inner // bc_w + 1)),
                  pl.BlockSpec((q, LANES), lambda b, c: (rows(b, c), 0)),
                  pl.BlockSpec((q, d_inner), lambda b, c: (rows(b, c), 0)),
                  pl.BlockSpec((1, LANES), lambda b, c: (0, 0)),
                  pl.BlockSpec((1, d_inner), lambda b, c: (0, 0)),
                  pl.BlockSpec((1, d_inner), lambda b, c: (0, 0)),
                  pl.BlockSpec((LANES, d_inner), lambda b, c: (0, 0))],
        out_specs=pl.BlockSpec((q, d_inner), lambda b, c: (rows(b, c), 0)),
        out_shape=jax.ShapeDtypeStruct((m, d_inner), BF16),
        scratch_shapes=[pltpu.VMEM((SSM_GROUPS, SSM_STATE, d_inner // SSM_GROUPS), F32),
                        pltpu.VMEM((2 * q + SUBLANES, d_inner), F32)],
        compiler_params=_params("parallel", "arbitrary"),
        name="ssd",
    )(xbc, xbc, xbc, dt, zs, a_log_pad, dskip_cols, g_norm, spread)


def _merge_body(yn_ref, uc_ref, ga_ref, gb_ref, wa_ref, wb_ref, lng_ref, lnb_ref, bo_ref, o_ref):
    for r0 in range(0, o_ref.shape[0], CHUNK):
        rows = slice(r0, r0 + CHUNK)
        u = uc_ref[rows, :].astype(F32)
        mu = jnp.mean(u, axis=-1, keepdims=True)
        ctr = u - mu
        var = jnp.mean(ctr * ctr, axis=-1, keepdims=True)
        un = _silu(ctr * lax.rsqrt(var + LN_EPS) * lng_ref[...] + lnb_ref[...]).astype(BF16)
        ya = jnp.dot(yn_ref[rows, :], wa_ref[...], preferred_element_type=F32)
        yb = jnp.dot(un, wb_ref[...], preferred_element_type=F32) + bo_ref[...]
        o_ref[rows, :] = (ga_ref[rows, :].astype(F32) * ya
                          + gb_ref[rows, :].astype(F32) * yb).astype(BF16)


def _merge(yn, uc, gates, wa, wb, ln_g, ln_b, b_out, bm=512):
    m, d = yn.shape
    row = pl.BlockSpec((bm, d), lambda i: (i, 0))
    vec = pl.BlockSpec((1, d), lambda i: (0, 0))
    wspec = pl.BlockSpec((d, d), lambda i: (0, 0), pipeline_mode=pl.Buffered(1))
    return pl.pallas_call(
        _merge_body,
        grid=(m // bm,),
        in_specs=[row, row, row, pl.BlockSpec((bm, d), lambda i: (i, 1)), wspec, wspec,
                  vec, vec, vec],
        out_specs=row,
        out_shape=jax.ShapeDtypeStruct((m, d), BF16),
        compiler_params=_params("parallel"),
        name="merge",
    )(yn, uc, gates, gates, wa, wb, ln_g, ln_b, b_out)


def _oproj_body(mg_ref, x_ref, w_ref, g2_ref, mod_ref, x1_ref, h2_ref):
    gate = mod_ref[0, 2:3, :]
    shift = mod_ref[0, 3:4, :]
    scale = mod_ref[0, 4:5, :]
    for r0 in range(0, x_ref.shape[0], CHUNK):
        rows = slice(r0, r0 + CHUNK)
        x1 = x_ref[rows, :] + gate * jnp.dot(mg_ref[rows, :], w_ref[...],
                                             preferred_element_type=F32)
        x1_ref[rows, :] = x1
        h2_ref[rows, :] = (_rms_rows(x1) * g2_ref[...] * (1.0 + scale) + shift).astype(BF16)


def _oproj(merged, x2, w_o, g2, mod3, seq, bm=512):
    m, d = x2.shape
    per = seq // bm
    row = pl.BlockSpec((bm, d), lambda i: (i, 0))
    return pl.pallas_call(
        _oproj_body,
        grid=(m // bm,),
        in_specs=[row, row,
                  pl.BlockSpec((d, d), lambda i: (0, 0), pipeline_mode=pl.Buffered(1)),
                  pl.BlockSpec((1, d), lambda i: (0, 0)),
                  pl.BlockSpec((1, N_MOD, d), lambda i: (i // per, 0, 0))],
        out_specs=[row, row],
        out_shape=[jax.ShapeDtypeStruct((m, d), F32), jax.ShapeDtypeStruct((m, d), BF16)],
        compiler_params=_params("parallel"),
        name="oproj",
    )(merged, x2, w_o, g2, mod3)


def _down_body(g_ref, x1_ref, w_ref, gf_ref, mod_ref, o_ref):
    gate = mod_ref[0, 5:6, :]
    x2 = x1_ref[...] + gate * jnp.dot(g_ref[...], w_ref[...], preferred_element_type=F32)
    o_ref[...] = _rms_rows(x2) * gf_ref[...]


def _down(gff, x1, w_down, g_final, mod3, seq, bm=256):
    m, d = x1.shape
    dff = gff.shape[1]
    per = seq // bm
    row = pl.BlockSpec((bm, d), lambda i: (i, 0))
    return pl.pallas_call(
        _down_body,
        grid=(m // bm,),
        in_specs=[pl.BlockSpec((bm, dff), lambda i: (i, 0)), row,
                  pl.BlockSpec((dff, d), lambda i: (0, 0), pipeline_mode=pl.Buffered(1)),
                  pl.BlockSpec((1, d), lambda i: (0, 0)),
                  pl.BlockSpec((1, N_MOD, d), lambda i: (i // per, 0, 0))],
        out_specs=row,
        out_shape=jax.ShapeDtypeStruct((m, d), F32),
        compiler_params=_params("parallel"),
        name="down",
    )(gff, x1, w_down, g_final, mod3)


def _layer(x, c, w_ada, b_ada, g_norm1, w_in, w_ssm_conv, b_ssm_conv, dt_bias, a_log, d_skip,
           g_ssm_norm, w_ssm_out, w_cfm_dw, b_cfm_dw, g_cfm_ln, b_cfm_ln, w_cfm_out, b_cfm_out,
           w_o, g_norm2, w_up, w_ff_conv, b_ff_conv, w_down, g_final):
    bsz, seq, d = x.shape
    d_inner = w_ssm_out.shape[0]
    heads = dt_bias.shape[0]
    xbc_dim = w_ssm_conv.shape[1]
    cfm_d = w_cfm_out.shape[0]
    d_ff = w_down.shape[0]
    row = lambda v: v.reshape(1, -1).astype(F32)

    o_xbc = d_inner
    o_dt = o_xbc + xbc_dim
    o_lin = o_dt + heads
    o_gate = o_lin + cfm_d
    o_g = o_gate + cfm_d
    w_in_b = w_in.astype(BF16)
    w_tail = w_in_b[:, o_lin:]
    w_up_b = w_up.astype(BF16)
    w_dt = jnp.pad(w_in[:, o_dt:o_lin], ((0, 0), (0, LANES - heads))).astype(BF16)
    dt_bias_pad = jnp.pad(row(dt_bias), ((0, 0), (0, LANES - heads)))
    a_log_pad = jnp.pad(row(a_log), ((0, 0), (0, LANES - heads)))
    dskip_cols = jnp.repeat(d_skip.astype(F32), SSM_HEAD_DIM).reshape(1, d_inner)

    x2 = x.reshape(bsz * seq, d)
    mod3 = _ada(c, w_ada, row(b_ada)).reshape(bsz, N_MOD, d)
    h = _norm1(x2, row(g_norm1), mod3, seq)

    xbc, zs = _seqmm_conv(h, [Cols(w_in_b, o_xbc, xbc_dim)], w_ssm_conv.astype(F32),
                          row(b_ssm_conv), "silu", seq, COL_BLOCK, "in_xbc_z",
                          rider=(Cols(w_in_b, 0, d_inner), _silu))
    dt = _seqmm_ew(h, Cols(w_dt, 0, LANES), dt_bias_pad, _softplus, F32, seq, LANES, "in_dt")
    uc, gates = _seqmm_conv(h, [Cols(w_tail, 0, cfm_d), Cols(w_tail, cfm_d, cfm_d)],
                            w_cfm_dw.astype(F32), row(b_cfm_dw), "glu", seq, COL_BLOCK // 2,
                            "in_cfm_gates",
                            rider=(Cols(w_tail, 2 * cfm_d, w_in.shape[1] - o_g), _sigmoid))

    yn = _ssd(xbc, dt, zs, a_log_pad, dskip_cols, row(g_ssm_norm), seq)
    merged = _merge(yn, uc, gates, w_ssm_out.astype(BF16), w_cfm_out.astype(BF16),
                    row(g_cfm_ln), row(b_cfm_ln), row(b_cfm_out))
    x1, h2 = _oproj(merged, x2, w_o.astype(BF16), row(g_norm2), mod3, seq)

    gff = _seqmm_conv(h2, [Cols(w_up_b, 0, d_ff), Cols(w_up_b, d_ff, d_ff)],
                      w_ff_conv.astype(F32), row(b_ff_conv), "gate", seq, COL_BLOCK, "ffn_up")
    out = _down(gff, x1, w_down.astype(BF16), row(g_final), mod3, seq)
    return out.reshape(bsz, seq, d)


def kernel(x, c, w_ada, b_ada, g_norm1, w_in, w_ssm_conv, b_ssm_conv, dt_bias, a_log, d_skip, g_ssm_norm, w_ssm_out, w_cfm_dw, b_cfm_dw, g_cfm_ln, b_cfm_ln, w_cfm_out, b_cfm_out, w_o, g_norm2, w_up, w_ff_conv, b_ff_conv, w_down, g_final):
    assert w_ada.shape[0] == 1, "single-layer stack"
    return _layer(x, c, w_ada[0], b_ada[0], g_norm1[0], w_in[0], w_ssm_conv[0], b_ssm_conv[0],
                  dt_bias[0], a_log[0], d_skip[0], g_ssm_norm[0], w_ssm_out[0], w_cfm_dw[0],
                  b_cfm_dw[0], g_cfm_ln[0], b_cfm_ln[0], w_cfm_out[0], b_cfm_out[0], w_o[0],
                  g_norm2[0], w_up[0], w_ff_conv[0], b_ff_conv[0], w_down[0], g_final)
```
